```python
import jax, jax.numpy as jnp
from jax import lax
import numpy as np

D_MODEL = 1024
BATCH = 16
SEQ = 4096
DEPTH = 4
DEC_BATCH = 8
DEC_SEQ = 16
PAST_LEN = 2048

CHUNK = 64
N_MEM = 256
D_LRU = 1024
N_LRU_BLOCKS = 8
LRU_BLOCK = D_LRU // N_LRU_BLOCKS
CONV_A_WIDTH = 4
LRU_C = 8.0
D_CONV = 512
CONV_B_WIDTH = 31
N_XATTN_HEADS = 4
XATTN_HEAD_DIM = 128
D_XATTN = N_XATTN_HEADS * XATTN_HEAD_DIM
N_BRANCHES = 3
D_FF = ((8 * D_MODEL + 3 * 256 - 1) // (3 * 256)) * 256
DEEPNORM_ALPHA = (2 * DEPTH) ** 0.25
DEEPNORM_BETA = (8 * DEPTH) ** -0.25
LN_EPS = 1e-5
COL_SPLITS = (D_LRU, 2 * D_LRU, 2 * D_LRU + 2 * D_CONV, 2 * D_LRU + 2 * D_CONV + D_XATTN)
D_IN = 2 * D_LRU + 2 * D_CONV + D_XATTN + N_BRANCHES * D_MODEL

kernel_name = "hawk_conformer_memory_streaming_step"


def layer_norm(x, g, b):
    xf = x.astype(jnp.float32)
    mu = xf.mean(-1, keepdims=True)
    var = jnp.square(xf - mu).mean(-1, keepdims=True)
    y = (xf - mu) * lax.rsqrt(var + LN_EPS)
    return (y * g.astype(jnp.float32) + b.astype(jnp.float32)).astype(x.dtype)


def causal_depthwise_conv(x_hist, w, b):
    c = w.shape[1]
    y = lax.conv_general_dilated(x_hist, w[:, None, :].astype(x_hist.dtype), window_strides=(1,), padding='VALID',
                                 dimension_numbers=('NWC', 'WIO', 'NWC'), feature_group_count=c)
    return y + b


def rg_lru(xa, h_prev, w_r, b_r, w_i, b_i, lam):
    bsz, t = xa.shape[0], xa.shape[1]
    xb = xa.reshape(bsz, t, N_LRU_BLOCKS, LRU_BLOCK)
    r = jax.nn.sigmoid(jnp.einsum('btnc,ncd->btnd', xb, w_r) + b_r).reshape(bsz, t, D_LRU)
    i = jax.nn.sigmoid(jnp.einsum('btnc,ncd->btnd', xb, w_i) + b_i).reshape(bsz, t, D_LRU)
    log_a = (LRU_C * r.astype(jnp.float32)) * jax.nn.log_sigmoid(lam.astype(jnp.float32))
    a = jnp.exp(log_a)
    u = jnp.sqrt(-jnp.expm1(2.0 * log_a)) * (i * xa).astype(jnp.float32)
    u = u.at[:, 0].add(a[:, 0] * h_prev.astype(jnp.float32))

    def combine(left, right):
        a1, b1 = left
        a2, b2 = right
        return a1 * a2, a2 * b1 + b2

    _, h = lax.associative_scan(combine, (a, u), axis=1)
    return h.astype(xa.dtype), h[:, -1].astype(xa.dtype)


def mem_cross_attention(q, k, v):
    bsz, t = q.shape[0], q.shape[1]
    s = jnp.einsum('bthd,bmhd->bhtm', q, k).astype(jnp.float32) * (XATTN_HEAD_DIM ** -0.5)
    p = jax.nn.softmax(s, axis=-1).astype(v.dtype)
    return jnp.einsum('bhtm,bmhd->bthd', p, v).reshape(bsz, t, D_XATTN)


def trunk_layer(x, mem_k, mem_v, conv_a_prev, h_prev, conv_b_prev, lw):
    (w_in, b_gate, conv_a_w, conv_a_b, lru_w_r, lru_b_r, lru_w_i, lru_b_i, lru_lambda, proj_a,
     conv_b_w, conv_b_b, ln_b_g, ln_b_b, proj_b, proj_c, w_out, ln1_g, ln1_b,
     w_ffn_gate, w_ffn_up, w_ffn_down, ln2_g, ln2_b) = lw
    bsz, t = x.shape[0], x.shape[1]
    z = x @ w_in
    xa, ya, glu, q, gates = jnp.split(z, COL_SPLITS, axis=-1)
    xa_hist = jnp.concatenate([conv_a_prev, xa], axis=1)
    xa_c = causal_depthwise_conv(xa_hist, conv_a_w, conv_a_b)
    h, h_last = rg_lru(xa_c, h_prev, lru_w_r, lru_b_r, lru_w_i, lru_b_i, lru_lambda)
    out_a = (h * jax.nn.gelu(ya)) @ proj_a
    u = glu[..., :D_CONV] * jax.nn.sigmoid(glu[..., D_CONV:])
    u_hist = jnp.concatenate([conv_b_prev, u], axis=1)
    v = causal_depthwise_conv(u_hist, conv_b_w, conv_b_b)
    out_b = jax.nn.silu(layer_norm(v, ln_b_g, ln_b_b)) @ proj_b
    out_c = mem_cross_attention(q.reshape(bsz, t, N_XATTN_HEADS, XATTN_HEAD_DIM), mem_k, mem_v) @ proj_c
    g = jax.nn.sigmoid(gates.reshape(bsz, t, N_BRANCHES, D_MODEL) + b_gate)
    merged = g[:, :, 0] * out_a + g[:, :, 1] * out_b + g[:, :, 2] * out_c
    x = layer_norm(DEEPNORM_ALPHA * x + merged @ w_out, ln1_g, ln1_b)
    ffn = (jax.nn.silu(x @ w_ffn_gate) * (x @ w_ffn_up)) @ w_ffn_down
    x = layer_norm(DEEPNORM_ALPHA * x + ffn, ln2_g, ln2_b)
    new_conv_a = xa_hist[:, -(CONV_A_WIDTH - 1):]
    new_conv_b = u_hist[:, -(CONV_B_WIDTH - 1):]
    return x, new_conv_a, h_last, new_conv_b


def setup_inputs(seed: int = 0) -> dict:
    key = jax.random.key(seed)
    ks = iter(jax.random.split(key, 48))
    nrm = lambda shape, scale: jax.random.normal(next(ks), shape, jnp.float32) * scale
    gain = lambda shape: 1.0 + nrm(shape, 0.02)
    lam_u = jax.random.uniform(next(ks), (DEPTH, D_LRU), jnp.float32, 0.9, 0.999) ** (1.0 / LRU_C)
    return {
        "x_prompt": nrm((BATCH, SEQ, D_MODEL), 1.0),
        "x_sample": nrm((DEC_BATCH, DEC_SEQ, D_MODEL), 1.0),
        "mem_prompt": nrm((BATCH, N_MEM, D_MODEL), 1.0),
        "state_conv_a": nrm((DEPTH, DEC_BATCH, CONV_A_WIDTH - 1, D_LRU), 1.0),
        "state_lru": nrm((DEPTH, DEC_BATCH, D_LRU), 0.5),
        "state_conv_b": nrm((DEPTH, DEC_BATCH, CONV_B_WIDTH - 1, D_CONV), 1.0),
        "cache_mem_k": nrm((DEPTH, DEC_BATCH, N_MEM, N_XATTN_HEADS, XATTN_HEAD_DIM), 1.0),
        "cache_mem_v": nrm((DEPTH, DEC_BATCH, N_MEM, N_XATTN_HEADS, XATTN_HEAD_DIM), 1.0),
        "ln_in_g": gain((D_MODEL,)),
        "ln_in_b": nrm((D_MODEL,), 0.02),
        "w_in": nrm((DEPTH, D_MODEL, D_IN), D_MODEL ** -0.5),
        "b_gate": nrm((DEPTH, N_BRANCHES, D_MODEL), 0.02),
        "conv_a_w": nrm((DEPTH, CONV_A_WIDTH, D_LRU), CONV_A_WIDTH ** -0.5),
        "conv_a_b": nrm((DEPTH, D_LRU), 0.02),
        "lru_w_r": nrm((DEPTH, N_LRU_BLOCKS, LRU_BLOCK, LRU_BLOCK), LRU_BLOCK ** -0.5),
        "lru_b_r": nrm((DEPTH, N_LRU_BLOCKS, LRU_BLOCK), 0.02),
        "lru_w_i": nrm((DEPTH, N_LRU_BLOCKS, LRU_BLOCK, LRU_BLOCK), LRU_BLOCK ** -0.5),
        "lru_b_i": nrm((DEPTH, N_LRU_BLOCKS, LRU_BLOCK), 0.02),
        "lru_lambda": jnp.log(lam_u) - jnp.log1p(-lam_u),
        "proj_a": nrm((DEPTH, D_LRU, D_MODEL), D_LRU ** -0.5),
        "conv_b_w": nrm((DEPTH, CONV_B_WIDTH, D_CONV), CONV_B_WIDTH ** -0.5),
        "conv_b_b": nrm((DEPTH, D_CONV), 0.02),
        "ln_b_g": gain((DEPTH, D_CONV)),
        "ln_b_b": nrm((DEPTH, D_CONV), 0.02),
        "proj_b": nrm((DEPTH, D_CONV, D_MODEL), D_CONV ** -0.5),
        "w_mem_k": nrm((DEPTH, D_MODEL, D_XATTN), D_MODEL ** -0.5),
        "w_mem_v": nrm((DEPTH, D_MODEL, D_XATTN), D_MODEL ** -0.5),
        "proj_c": nrm((DEPTH, D_XATTN, D_MODEL), D_XATTN ** -0.5),
        "w_out": nrm((DEPTH, D_MODEL, D_MODEL), DEEPNORM_BETA * D_MODEL ** -0.5),
        "ln1_g": gain((DEPTH, D_MODEL)),
        "ln1_b": nrm((DEPTH, D_MODEL), 0.02),
        "w_ffn_gate": nrm((DEPTH, D_MODEL, D_FF), D_MODEL ** -0.5),
        "w_ffn_up": nrm((DEPTH, D_MODEL, D_FF), D_MODEL ** -0.5),
        "w_ffn_down": nrm((DEPTH, D_FF, D_MODEL), DEEPNORM_BETA * D_FF ** -0.5),
        "ln2_g": gain((DEPTH, D_MODEL)),
        "ln2_b": nrm((DEPTH, D_MODEL), 0.02),
    }


def reference(x_prompt, x_sample, mem_prompt, state_conv_a, state_lru, state_conv_b, cache_mem_k, cache_mem_v,
              ln_in_g, ln_in_b, w_in, b_gate, conv_a_w, conv_a_b, lru_w_r, lru_b_r, lru_w_i, lru_b_i, lru_lambda,
              proj_a, conv_b_w, conv_b_b, ln_b_g, ln_b_b, proj_b, w_mem_k, w_mem_v, proj_c, w_out, ln1_g, ln1_b,
              w_ffn_gate, w_ffn_up, w_ffn_down, ln2_g, ln2_b):
    xp = layer_norm(x_prompt, ln_in_g, ln_in_b)
    xs = layer_norm(x_sample, ln_in_g, ln_in_b)
    bp = xp.shape[0]
    dt = xp.dtype
    zero_conv_a = jnp.zeros((bp, CONV_A_WIDTH - 1, D_LRU), dt)
    zero_h = jnp.zeros((bp, D_LRU), dt)
    zero_conv_b = jnp.zeros((bp, CONV_B_WIDTH - 1, D_CONV), dt)
    ca_p, h_p, cb_p, mk_p, mv_p = [], [], [], [], []
    ca_s, h_s, cb_s = [], [], []
    for l in range(DEPTH):
        lw = (w_in[l], b_gate[l], conv_a_w[l], conv_a_b[l], lru_w_r[l], lru_b_r[l], lru_w_i[l], lru_b_i[l],
              lru_lambda[l], proj_a[l], conv_b_w[l], conv_b_b[l], ln_b_g[l], ln_b_b[l], proj_b[l], proj_c[l],
              w_out[l], ln1_g[l], ln1_b[l], w_ffn_gate[l], w_ffn_up[l], w_ffn_down[l], ln2_g[l], ln2_b[l])
        mem_k = (mem_prompt @ w_mem_k[l]).reshape(bp, N_MEM, N_XATTN_HEADS, XATTN_HEAD_DIM)
        mem_v = (mem_prompt @ w_mem_v[l]).reshape(bp, N_MEM, N_XATTN_HEADS, XATTN_HEAD_DIM)
        xp, ca, hl, cb = trunk_layer(xp, mem_k, mem_v, zero_conv_a, zero_h, zero_conv_b, lw)
        ca_p.append(ca); h_p.append(hl); cb_p.append(cb); mk_p.append(mem_k); mv_p.append(mem_v)
        xs, ca, hl, cb = trunk_layer(xs, cache_mem_k[l], cache_mem_v[l], state_conv_a[l], state_lru[l],
                                     state_conv_b[l], lw)
        ca_s.append(ca); h_s.append(hl); cb_s.append(cb)
    new_conv_a_prompt = jnp.stack(ca_p)
    new_lru_prompt = jnp.stack(h_p)
    new_conv_b_prompt = jnp.stack(cb_p)
    new_mem_k_prompt = jnp.stack(mk_p)
    new_mem_v_prompt = jnp.stack(mv_p)
    new_conv_a_sample = jnp.stack(ca_s)
    new_lru_sample = jnp.stack(h_s)
    new_conv_b_sample = jnp.stack(cb_s)
    return (xp, xs, new_conv_a_prompt, new_lru_prompt, new_conv_b_prompt, new_mem_k_prompt, new_mem_v_prompt,
            new_conv_a_sample, new_lru_sample, new_conv_b_sample)
```

```python
import functools

import jax
import jax.numpy as jnp
from jax import lax
from jax.experimental import pallas as pl
from jax.experimental.pallas import tpu as pltpu

D_MODEL = 1024
N_MEM = 256
D_LRU = 1024
N_LRU_BLOCKS = 8
LRU_BLOCK = D_LRU // N_LRU_BLOCKS
CONV_A_WIDTH = 4
LRU_C = 8.0
D_CONV = 512
CONV_B_WIDTH = 31
N_HEADS = 4
HEAD_DIM = 128
D_XATTN = N_HEADS * HEAD_DIM
LN_EPS = 1e-5

C_XA = 0
C_YA = D_LRU
C_GLU = 2 * D_LRU
C_Q = 2 * D_LRU + 2 * D_CONV
C_GATE = C_Q + D_XATTN

SUBLANES = 8
HIST_A_PAD = 8
HIST_B_PAD = 32
CONV_ROWS = 16
VMEM_LIMIT = 52 * 1024 * 1024

BF16 = jnp.bfloat16
F32 = jnp.float32


def _dot(a, b):
    return jnp.dot(a, b, preferred_element_type=F32)


def _sigmoid(x):
    return 1.0 / (1.0 + jnp.exp(-x))


def _layer_norm(x, g, b):
    mu = jnp.mean(x, axis=-1, keepdims=True)
    xc = x - mu
    var = jnp.mean(xc * xc, axis=-1, keepdims=True)
    return xc * lax.rsqrt(var + LN_EPS) * g + b


def _gelu_tanh(x):
    return 0.5 * x * (1.0 + jnp.tanh(0.7978845608028654 * (x + 0.044715 * (x * x * x))))


def _causal_conv(buf_ref, s, w_ref, bias, out_ref, row0, n_rows, width, first):
    n_ch = buf_ref.shape[-1]
    halo = ((first + width - 1 + SUBLANES - 1) // SUBLANES) * SUBLANES
    rows = min(CONV_ROWS, n_rows)

    def body(i, carry):
        base = pl.multiple_of(i * rows, SUBLANES)
        win = buf_ref[s, pl.ds(base, rows + halo), :]
        acc = jnp.broadcast_to(bias, (rows, n_ch))
        for k in range(width):
            acc = acc + w_ref[k:k + 1, :] * win[first + k:first + k + rows, :]
        out_ref[pl.ds(pl.multiple_of(row0 + base, SUBLANES), rows), :] = acc
        return carry

    lax.fori_loop(0, n_rows // rows, body, 0)


def _lru_scan(a_ref, u_ref, row0, n_rows, h0):
    n_ch = a_ref.shape[-1]
    row = lax.broadcasted_iota(jnp.int32, (SUBLANES, n_ch), 0)

    def body(i, h):
        off = pl.multiple_of(row0 + i * SUBLANES, SUBLANES)
        a = a_ref[pl.ds(off, SUBLANES), :]
        b = u_ref[pl.ds(off, SUBLANES), :]
        for sft in (1, 2, 4):
            keep = row >= sft
            a_sh = jnp.where(keep, pltpu.roll(a, sft, 0), 1.0)
            b_sh = jnp.where(keep, pltpu.roll(b, sft, 0), 0.0)
            b = a * b_sh + b
            a = a * a_sh
        hc = a * h + b
        u_ref[pl.ds(off, SUBLANES), :] = hc
        return hc[SUBLANES - 1:SUBLANES, :]

    return lax.fori_loop(0, n_rows // SUBLANES, body, h0)


def _mixer_kernel(x_ref, k_ref, v_ref, ca0_ref, h0_ref, cb0_ref,
                  w_in_ref, bg_ref, caw_ref, cab_ref, wri_ref, br_ref, bi_ref, lam_ref, pa_ref,
                  cbw_ref, cbb_ref, lnbg_ref, lnbb_ref, pb_ref, pc_ref, wo_ref, ln1g_ref, ln1b_ref,
                  x1_ref, ca_out, h_out, cb_out,
                  bufa, bufb, hst, a_s, u_s, xc_s, vb_s, att_s, *, n_seq, seq_rows, alpha):
    t = pl.program_id(1)
    last_t = pl.num_programs(1) - 1
    L = seq_rows

    @pl.when(t == 0)
    def _():
        for s in range(n_seq):
            bufa[s, HIST_A_PAD - 3:HIST_A_PAD, :] = ca0_ref[s]
            bufb[s, HIST_B_PAD - 30:HIST_B_PAD, :] = cb0_ref[s]
            hst[s] = h0_ref[s]

    x = x_ref[...]
    xb = x.astype(BF16)

    xa = _dot(xb, w_in_ref[:, C_XA:C_XA + D_LRU])
    for s in range(n_seq):
        bufa[s, HIST_A_PAD:HIST_A_PAD + L, :] = xa[s * L:(s + 1) * L, :]
        _causal_conv(bufa, s, caw_ref, cab_ref[...], xc_s, s * L, L, CONV_A_WIDTH, HIST_A_PAD - 3)
        new_hist = bufa[s, L + HIST_A_PAD - 3:L + HIST_A_PAD, :]
        bufa[s, HIST_A_PAD - 3:HIST_A_PAD, :] = new_hist
    xc = xc_s[...]
    xcb = xc.astype(BF16)
    lam = lam_ref[...]
    log_sig_lam = jnp.minimum(lam, 0.0) - jnp.log1p(jnp.exp(-jnp.abs(lam)))
    for n in range(N_LRU_BLOCKS):
        c0, c1 = n * LRU_BLOCK, (n + 1) * LRU_BLOCK
        ri = _dot(xcb[:, c0:c1], wri_ref[n])
        r = _sigmoid(ri[:, :LRU_BLOCK] + br_ref[:, c0:c1])
        gi = _sigmoid(ri[:, LRU_BLOCK:] + bi_ref[:, c0:c1])
        a = jnp.exp((LRU_C * r) * log_sig_lam[:, c0:c1])
        a_s[:, c0:c1] = a
        u_s[:, c0:c1] = jnp.sqrt(1.0 - a * a) * (gi * xc[:, c0:c1])
    for s in range(n_seq):
        h_last = _lru_scan(a_s, u_s, s * L, L, hst[s])
        hst[s] = h_last
    ya = _dot(xb, w_in_ref[:, C_YA:C_YA + D_LRU])
    out_a = _dot((u_s[...] * _gelu_tanh(ya)).astype(BF16), pa_ref[...])
    g0 = _sigmoid(_dot(xb, w_in_ref[:, C_GATE:C_GATE + D_MODEL]) + bg_ref[0:1, :])
    merged = g0 * out_a

    glu = _dot(xb, w_in_ref[:, C_GLU:C_GLU + 2 * D_CONV])
    ub = glu[:, :D_CONV] * _sigmoid(glu[:, D_CONV:])
    for s in range(n_seq):
        bufb[s, HIST_B_PAD:HIST_B_PAD + L, :] = ub[s * L:(s + 1) * L, :]
        _causal_conv(bufb, s, cbw_ref, cbb_ref[...], vb_s, s * L, L, CONV_B_WIDTH, HIST_B_PAD - 30)
        new_hist = bufb[s, L + HIST_B_PAD - 30:L + HIST_B_PAD, :]
        bufb[s, HIST_B_PAD - 30:HIST_B_PAD, :] = new_hist
    vn = _layer_norm(vb_s[...], lnbg_ref[...], lnbb_ref[...])
    out_b = _dot((vn * _sigmoid(vn)).astype(BF16), pb_ref[...])
    g1 = _sigmoid(_dot(xb, w_in_ref[:, C_GATE + D_MODEL:C_GATE + 2 * D_MODEL]) + bg_ref[1:2, :])
    merged = merged + g1 * out_b

    q = _dot(xb, w_in_ref[:, C_Q:C_Q + D_XATTN]).astype(BF16)
    for s in range(n_seq):
        for hd in range(N_HEADS):
            c0, c1 = hd * HEAD_DIM, (hd + 1) * HEAD_DIM
            sc = lax.dot_general(q[s * L:(s + 1) * L, c0:c1], k_ref[s, :, c0:c1],
                                 (((1,), (1,)), ((), ())), preferred_element_type=F32)
            sc = sc * (HEAD_DIM ** -0.5)
            p = jnp.exp(sc - jnp.max(sc, axis=-1, keepdims=True))
            p = p / jnp.sum(p, axis=-1, keepdims=True)
            att_s[s * L:(s + 1) * L, c0:c1] = _dot(p.astype(BF16), v_ref[s, :, c0:c1])
    out_c = _dot(att_s[...].astype(BF16), pc_ref[...])
    g2 = _sigmoid(_dot(xb, w_in_ref[:, C_GATE + 2 * D_MODEL:C_GATE + 3 * D_MODEL]) + bg_ref[2:3, :])
    merged = merged + g2 * out_c

    y = _dot(merged.astype(BF16), wo_ref[...])
    x1_ref[...] = _layer_norm(alpha * x + y, ln1g_ref[...], ln1b_ref[...])

    @pl.when(t == last_t)
    def _():
        for s in range(n_seq):
            ca_out[s] = bufa[s, HIST_A_PAD - 3:HIST_A_PAD, :]
            cb_out[s] = bufb[s, HIST_B_PAD - 30:HIST_B_PAD, :]
            h_out[s] = hst[s]


def _ffn_kernel(x_ref, wg_ref, wu_ref, wd_ref, g_ref, b_ref, o_ref, *, alpha):
    x = x_ref[...]
    xb = x.astype(BF16)
    gate = _dot(xb, wg_ref[...])
    up = _dot(xb, wu_ref[...])
    hid = (gate * _sigmoid(gate) * up).astype(BF16)
    o_ref[...] = _layer_norm(alpha * x + _dot(hid, wd_ref[...]), g_ref[...], b_ref[...])


def _ln_kernel(x_ref, g_ref, b_ref, o_ref):
    o_ref[...] = _layer_norm(x_ref[...], g_ref[...], b_ref[...])


def _kv_kernel(m_ref, wk_ref, wv_ref, k_ref, v_ref, kb_ref, vb_ref):
    mb = m_ref[...].astype(BF16)
    k = _dot(mb, wk_ref[...])
    v = _dot(mb, wv_ref[...])
    k_ref[...] = k
    v_ref[...] = v
    kb_ref[...] = k.astype(BF16)
    vb_ref[...] = v.astype(BF16)


def _const_spec(shape, layer=None):
    if layer is None:
        return pl.BlockSpec(shape, lambda *_: (0,) * len(shape), pipeline_mode=pl.Buffered(1))
    return pl.BlockSpec((None,) + shape, lambda *_: (layer,) + (0,) * len(shape), pipeline_mode=pl.Buffered(1))


def _row_tile(n_rows, want):
    tile = min(want, n_rows)
    assert n_rows % tile == 0 and tile % SUBLANES == 0
    return tile


def _layer_norm_call(x2d, g, b):
    rows = x2d.shape[0]
    tile = _row_tile(rows, 512)
    return pl.pallas_call(
        _ln_kernel,
        grid=(rows // tile,),
        in_specs=[pl.BlockSpec((tile, D_MODEL), lambda i: (i, 0)),
                  _const_spec((1, D_MODEL)), _const_spec((1, D_MODEL))],
        out_specs=pl.BlockSpec((tile, D_MODEL), lambda i: (i, 0)),
        out_shape=jax.ShapeDtypeStruct(x2d.shape, F32),
        compiler_params=pltpu.CompilerParams(dimension_semantics=("arbitrary",)),
        name="ln_in",
    )(x2d, g.reshape(1, D_MODEL), b.reshape(1, D_MODEL))


def _kv_call(mem, wk, wv):
    depth = wk.shape[0]
    bsz = mem.shape[0]
    out_f32 = jax.ShapeDtypeStruct((depth, bsz, N_MEM, D_XATTN), F32)
    out_bf16 = jax.ShapeDtypeStruct((depth, bsz, N_MEM, D_XATTN), BF16)
    w_spec = pl.BlockSpec((None, D_MODEL, D_XATTN), lambda l, b: (l, 0, 0))
    o_spec = pl.BlockSpec((None, None, N_MEM, D_XATTN), lambda l, b: (l, b, 0, 0))
    return pl.pallas_call(
        _kv_kernel,
        grid=(depth, bsz),
        in_specs=[pl.BlockSpec((None, N_MEM, D_MODEL), lambda l, b: (b, 0, 0)), w_spec, w_spec],
        out_specs=[o_spec, o_spec, o_spec, o_spec],
        out_shape=[out_f32, out_f32, out_bf16, out_bf16],
        compiler_params=pltpu.CompilerParams(dimension_semantics=("arbitrary", "arbitrary")),
        name="mem_kv",
    )(mem, wk, wv)


def _mixer_call(x2d, kb, vb, ca0, h0, cb0, lw, layer, *, bsz, seq, n_seq, seq_rows, alpha):
    L = seq_rows
    tile = n_seq * L
    n_t = seq // L
    assert bsz % n_seq == 0 and seq % L == 0 and (n_seq == 1 or n_t == 1)
    assert L % CONV_ROWS == 0 or L < CONV_ROWS
    kernel = functools.partial(_mixer_kernel, n_seq=n_seq, seq_rows=L, alpha=alpha)
    x_spec = pl.BlockSpec((tile, D_MODEL), lambda b, t: (b * n_t + t, 0))

    def per_seq(shape):
        return pl.BlockSpec((n_seq,) + shape, lambda b, t: (b,) + (0,) * len(shape))

    in_specs = [
        x_spec, per_seq((N_MEM, D_XATTN)), per_seq((N_MEM, D_XATTN)),
        per_seq((CONV_A_WIDTH - 1, D_LRU)), per_seq((1, D_LRU)), per_seq((CONV_B_WIDTH - 1, D_CONV)),
        _const_spec((D_MODEL, lw["w_in"].shape[-1]), layer),
        _const_spec((3, D_MODEL), layer),
        _const_spec((CONV_A_WIDTH, D_LRU), layer),
        _const_spec((1, D_LRU), layer),
        _const_spec((N_LRU_BLOCKS, LRU_BLOCK, 2 * LRU_BLOCK), layer),
        _const_spec((1, D_LRU), layer),
        _const_spec((1, D_LRU), layer),
        _const_spec((1, D_LRU), layer),
        _const_spec((D_LRU, D_MODEL), layer),
        _const_spec((CONV_B_WIDTH, D_CONV), layer),
        _const_spec((1, D_CONV), layer),
        _const_spec((1, D_CONV), layer),
        _const_spec((1, D_CONV), layer),
        _const_spec((D_CONV, D_MODEL), layer),
        _const_spec((D_XATTN, D_MODEL), layer),
        _const_spec((D_MODEL, D_MODEL), layer),
        _const_spec((1, D_MODEL), layer),
        _const_spec((1, D_MODEL), layer),
    ]
    out_specs = [x_spec, per_seq((CONV_A_WIDTH - 1, D_LRU)), per_seq((1, D_LRU)),
                 per_seq((CONV_B_WIDTH - 1, D_CONV))]
    out_shape = [jax.ShapeDtypeStruct(x2d.shape, F32),
                 jax.ShapeDtypeStruct((bsz, CONV_A_WIDTH - 1, D_LRU), F32),
                 jax.ShapeDtypeStruct((bsz, 1, D_LRU), F32),
                 jax.ShapeDtypeStruct((bsz, CONV_B_WIDTH - 1, D_CONV), F32)]
    scratch = [
        pltpu.VMEM((n_seq, HIST_A_PAD + L, D_LRU), F32),
        pltpu.VMEM((n_seq, HIST_B_PAD + L, D_CONV), F32),
        pltpu.VMEM((n_seq, 1, D_LRU), F32),
        pltpu.VMEM((tile, D_LRU), F32),
        pltpu.VMEM((tile, D_LRU), F32),
        pltpu.VMEM((tile, D_LRU), F32),
        pltpu.VMEM((tile, D_CONV), F32),
        pltpu.VMEM((tile, D_XATTN), F32),
    ]
    return pl.pallas_call(
        kernel,
        grid=(bsz // n_seq, n_t),
        in_specs=in_specs,
        out_specs=out_specs,
        out_shape=out_shape,
        scratch_shapes=scratch,
        compiler_params=pltpu.CompilerParams(dimension_semantics=("arbitrary", "arbitrary"),
                                             vmem_limit_bytes=VMEM_LIMIT),
        name="mixer",
    )(x2d, kb, vb, ca0, h0, cb0,
      lw["w_in"], lw["b_gate"], lw["conv_a_w"], lw["conv_a_b"], lw["w_ri"], lw["lru_b_r"], lw["lru_b_i"],
      lw["lru_lambda"], lw["proj_a"], lw["conv_b_w"], lw["conv_b_b"], lw["ln_b_g"], lw["ln_b_b"],
      lw["proj_b"], lw["proj_c"], lw["w_out"], lw["ln1_g"], lw["ln1_b"])


def _ffn_call(x2d, lw, layer, *, tile, alpha):
    rows = x2d.shape[0]
    d_ff = lw["w_ffn_gate"].shape[-1]
    x_spec = pl.BlockSpec((tile, D_MODEL), lambda i: (i, 0))
    return pl.pallas_call(
        functools.partial(_ffn_kernel, alpha=alpha),
        grid=(rows // tile,),
        in_specs=[x_spec,
                  _const_spec((D_MODEL, d_ff), layer), _const_spec((D_MODEL, d_ff), layer),
                  _const_spec((d_ff, D_MODEL), layer),
                  _const_spec((1, D_MODEL), layer), _const_spec((1, D_MODEL), layer)],
        out_specs=x_spec,
        out_shape=jax.ShapeDtypeStruct(x2d.shape, F32),
        compiler_params=pltpu.CompilerParams(dimension_semantics=("arbitrary",),
                                             vmem_limit_bytes=VMEM_LIMIT),
        name="ffn",
    )(x2d, lw["w_ffn_gate"], lw["w_ffn_up"], lw["w_ffn_down"], lw["ln2_g"], lw["ln2_b"])


def kernel(x_prompt, x_sample, mem_prompt, state_conv_a, state_lru, state_conv_b, cache_mem_k, cache_mem_v,
           ln_in_g, ln_in_b, w_in, b_gate, conv_a_w, conv_a_b, lru_w_r, lru_b_r, lru_w_i, lru_b_i, lru_lambda,
           proj_a, conv_b_w, conv_b_b, ln_b_g, ln_b_b, proj_b, w_mem_k, w_mem_v, proj_c, w_out, ln1_g, ln1_b,
           w_ffn_gate, w_ffn_up, w_ffn_down, ln2_g, ln2_b):
    depth = w_in.shape[0]
    bp, tp, _ = x_prompt.shape
    bs, ts, _ = x_sample.shape
    alpha = (2 * depth) ** 0.25

    row = lambda p: p.reshape(depth, 1, p.shape[-1])
    lw = {
        "w_in": w_in.astype(BF16), "b_gate": b_gate, "conv_a_w": conv_a_w, "conv_a_b": row(conv_a_b),
        "w_ri": jnp.concatenate([lru_w_r, lru_w_i], axis=-1).astype(BF16),
        "lru_b_r": lru_b_r.reshape(depth, 1, D_LRU), "lru_b_i": lru_b_i.reshape(depth, 1, D_LRU),
        "lru_lambda": row(lru_lambda), "proj_a": proj_a.astype(BF16),
        "conv_b_w": conv_b_w, "conv_b_b": row(conv_b_b), "ln_b_g": row(ln_b_g), "ln_b_b": row(ln_b_b),
        "proj_b": proj_b.astype(BF16), "proj_c": proj_c.astype(BF16), "w_out": w_out.astype(BF16),
        "ln1_g": row(ln1_g), "ln1_b": row(ln1_b),
        "w_ffn_gate": w_ffn_gate.astype(BF16), "w_ffn_up": w_ffn_up.astype(BF16),
        "w_ffn_down": w_ffn_down.astype(BF16), "ln2_g": row(ln2_g), "ln2_b": row(ln2_b),
    }

    mem_k, mem_v, mem_kb, mem_vb = _kv_call(mem_prompt, w_mem_k.astype(BF16), w_mem_v.astype(BF16))
    cache_kb = cache_mem_k.reshape(depth, bs, N_MEM, D_XATTN).astype(BF16)
    cache_vb = cache_mem_v.reshape(depth, bs, N_MEM, D_XATTN).astype(BF16)

    xp = _layer_norm_call(x_prompt.reshape(bp * tp, D_MODEL), ln_in_g, ln_in_b)
    xs = _layer_norm_call(x_sample.reshape(bs * ts, D_MODEL), ln_in_g, ln_in_b)

    zero_ca = jnp.zeros((bp, CONV_A_WIDTH - 1, D_LRU), F32)
    zero_h = jnp.zeros((bp, 1, D_LRU), F32)
    zero_cb = jnp.zeros((bp, CONV_B_WIDTH - 1, D_CONV), F32)
    lp = _row_tile(tp, 256)

    ca_p, h_p, cb_p, ca_s, h_s, cb_s = [], [], [], [], [], []
    for l in range(depth):
        xp, ca, hl, cb = _mixer_call(xp, mem_kb[l], mem_vb[l], zero_ca, zero_h, zero_cb, lw, l,
                                     bsz=bp, seq=tp, n_seq=1, seq_rows=lp, alpha=alpha)
        xp = _ffn_call(xp, lw, l, tile=_row_tile(bp * tp, 512), alpha=alpha)
        ca_p.append(ca); h_p.append(hl[:, 0]); cb_p.append(cb)
        xs, ca, hl, cb = _mixer_call(xs, cache_kb[l], cache_vb[l], state_conv_a[l], state_lru[l][:, None, :],
                                     state_conv_b[l], lw, l, bsz=bs, seq=ts, n_seq=bs, seq_rows=ts, alpha=alpha)
        xs = _ffn_call(xs, lw, l, tile=bs * ts, alpha=alpha)
        ca_s.append(ca); h_s.append(hl[:, 0]); cb_s.append(cb)

    kv_shape = (depth, bp, N_MEM, N_HEADS, HEAD_DIM)
    return (xp.reshape(bp, tp, D_MODEL), xs.reshape(bs, ts, D_MODEL),
            jnp.stack(ca_p), jnp.stack(h_p), jnp.stack(cb_p),
            mem_k.reshape(kv_shape), mem_v.reshape(kv_shape),
            jnp.stack(ca_s), jnp.stack(h_s), jnp.stack(cb_s))
```

```python
import functools

import jax
import jax.numpy as jnp
from jax import lax
from jax.experimental import pallas as pl
from jax.experimental.pallas import tpu as pltpu

D_MODEL = 1024
N_MEM = 256
D_LRU = 1024
N_LRU_BLOCKS = 8
LRU_BLOCK = D_LRU // N_LRU_BLOCKS
CONV_A_WIDTH = 4
LRU_C = 8.0
D_CONV = 512
CONV_B_WIDTH = 31
N_HEADS = 4
HEAD_DIM = 128
D_XATTN = N_HEADS * HEAD_DIM
LN_EPS = 1e-5

C_XA = 0
C_YA = D_LRU
C_GLU = 2 * D_LRU
C_Q = 2 * D_LRU + 2 * D_CONV
C_GATE = C_Q + D_XATTN

SUBLANES = 8
LANES = 128
HIST_A = CONV_A_WIDTH - 1
HIST_B = CONV_B_WIDTH - 1
CONV_CHUNK = 8
TAP_BLOCK = 8
MXU_COLS = 256
VMEM_LIMIT = 52 * 1024 * 1024

BF16 = jnp.bfloat16
F32 = jnp.float32


def _dot(a, b):
    return jnp.dot(a, b, preferred_element_type=F32)


def _sigmoid(x):
    return 0.5 * jnp.tanh(0.5 * x) + 0.5


def _layer_norm(x, g, b):
    mu = jnp.mean(x, axis=-1, keepdims=True)
    xc = x - mu
    var = jnp.mean(xc * xc, axis=-1, keepdims=True)
    return xc * lax.rsqrt(var + LN_EPS) * g + b


def _gelu_tanh(x):
    return x * (0.5 + 0.5 * jnp.tanh(x * (0.7978845608028654 + 0.035677408136300125 * (x * x))))


def _vrow(j, n=1):
    return slice(SUBLANES * j, SUBLANES * (j + n))


def _sublane_index(n_ch):
    return lax.broadcasted_iota(jnp.int32, (SUBLANES, n_ch), 0)


def _fill_halo(ext, hist, n_hist, n_vrows, chained):
    sub = _sublane_index(ext.shape[-1])
    for i in range(n_hist):
        prev = hist[_vrow(i), :]
        if chained:
            cur = ext[_vrow(n_vrows + i), :]
            ext[_vrow(i), :] = pltpu.roll(jnp.where(sub == SUBLANES - 1, prev, cur), 1, 0)
        else:
            ext[_vrow(i), :] = prev
    for i in range(n_hist):
        hist[_vrow(i), :] = ext[_vrow(n_vrows + i), :]


def _conv_pieces(ext, w_ref, b_ref, out_ref, n_vrows, width):
    def piece(g, j0, ch):
        lanes = slice(g * LANES, (g + 1) * LANES)
        accs = [b_ref[:, lanes]] * ch
        for k0 in range(0, width, TAP_BLOCK):
            kn = min(TAP_BLOCK, width - k0)
            rows = [ext[_vrow(j0 + k0 + m), lanes] for m in range(ch + kn - 1)]
            for k in range(kn):
                wk = w_ref[_vrow(k0 + k), lanes]
                accs = [accs[i] + wk * rows[i + k] for i in range(ch)]
        for i in range(ch):
            out_ref[_vrow(j0 + i), lanes] = accs[i]

    return [functools.partial(piece, g, j0, min(CONV_CHUNK, n_vrows - j0))
            for g in range(ext.shape[-1] // LANES) for j0 in range(0, n_vrows, CONV_CHUNK)]


def _scan_pieces(a_s, u_s, hcar, n_vrows, chained):
    def piece(g):
        sub = _sublane_index(LANES)
        lanes = slice(g * LANES, (g + 1) * LANES)
        h = hcar[:, lanes]
        if chained:
            h = jnp.where(sub == 0, pltpu.roll(h, 1, 0), 0.0)
        prod = None
        for j in range(n_vrows):
            a = a_s[_vrow(j), lanes]
            h = a * h + u_s[_vrow(j), lanes]
            u_s[_vrow(j), lanes] = h
            if chained:
                prod = a if prod is None else prod * a
                a_s[_vrow(j), lanes] = prod
        if chained:
            seg_a, seg_h = prod, h
            for sft in (1, 2, 4):
                keep = sub >= sft
                a_sh = jnp.where(keep, pltpu.roll(seg_a, sft, 0), 1.0)
                h_sh = jnp.where(keep, pltpu.roll(seg_h, sft, 0), 0.0)
                seg_h = seg_a * h_sh + seg_h
                seg_a = seg_a * a_sh
            h_in = jnp.where(sub >= 1, pltpu.roll(seg_h, 1, 0), 0.0)
            for j in range(n_vrows):
                u_s[_vrow(j), lanes] = u_s[_vrow(j), lanes] + a_s[_vrow(j), lanes] * h_in
            h = seg_h
        hcar[:, lanes] = h

    return [functools.partial(piece, g) for g in range(a_s.shape[-1] // LANES)]


def _interleave(mxu_pieces, vpu_pieces):
    n_m, n_v = len(mxu_pieces), len(vpu_pieces)
    for i, piece in enumerate(mxu_pieces):
        piece()
        for vp in vpu_pieces[i * n_v // n_m:(i + 1) * n_v // n_m]:
            vp()


def _attend(q, k, v):
    sc = lax.dot_general(q, k, (((1,), (1,)), ((), ())), preferred_element_type=F32) * (HEAD_DIM ** -0.5)
    p = jnp.exp(sc - jnp.max(sc, axis=-1, keepdims=True))
    return _dot(p.astype(BF16), v) / jnp.sum(p, axis=-1, keepdims=True)


def _mixer_kernel(x_ref, k_ref, v_ref, ca0_ref, h0_ref, cb0_ref,
                  w_in_ref, bg_ref, caw_ref, cab_ref, wri_ref, br_ref, bi_ref, lam_ref, pa_ref,
                  cbw_ref, cbb_ref, lnbg_ref, lnbb_ref, pb_ref, pc_ref, wo_ref, ln1g_ref, ln1b_ref,
                  x1_ref, ca_out, h_out, cb_out,
                  exta, extb, hista, histb, hcar, a_s, u_s, xc_s, vb_s, att_s, xb_s, q_s, z_s,
                  *, chained, n_vrows, alpha):
    t = pl.program_id(1)
    last_t = pl.num_programs(1) - 1
    S = n_vrows

    @pl.when(t == 0)
    def _():
        hista[...] = ca0_ref[...]
        histb[...] = cb0_ref[...]
        hcar[...] = h0_ref[...]

    xb_s[...] = x_ref[...].astype(BF16)

    def proj(c0, width):
        return _dot(xb_s[...], w_in_ref[:, c0:c0 + width])

    def proj_pieces(dst, dst0, c0, width):
        def piece(off):
            dst[:, dst0 + off:dst0 + off + MXU_COLS] = proj(c0 + off, MXU_COLS).astype(dst.dtype)
        return [functools.partial(piece, off) for off in range(0, width, MXU_COLS)]

    exta[_vrow(HIST_A, S), :] = proj(C_XA, D_LRU)
    glu = proj(C_GLU, 2 * D_CONV)
    extb[_vrow(HIST_B, S), :] = glu[:, :D_CONV] * _sigmoid(glu[:, D_CONV:])
    _fill_halo(exta, hista, HIST_A, S, chained)
    _fill_halo(extb, histb, HIST_B, S, chained)

    _interleave(proj_pieces(q_s, 0, C_Q, D_XATTN), _conv_pieces(exta, caw_ref, cab_ref, xc_s, S, CONV_A_WIDTH))

    lam = lam_ref[...]
    log_sig_lam = jnp.minimum(lam, 0.0) - jnp.log1p(jnp.exp(-jnp.abs(lam)))
    for n in range(N_LRU_BLOCKS):
        c0, c1 = n * LRU_BLOCK, (n + 1) * LRU_BLOCK
        xc = xc_s[:, c0:c1]
        ri = _dot(xc.astype(BF16), wri_ref[n])
        r = _sigmoid(ri[:, :LRU_BLOCK] + br_ref[:, c0:c1])
        gi = _sigmoid(ri[:, LRU_BLOCK:] + bi_ref[:, c0:c1])
        a = jnp.exp((LRU_C * r) * log_sig_lam[:, c0:c1])
        a_s[:, c0:c1] = a
        u_s[:, c0:c1] = jnp.sqrt(1.0 - a * a) * (gi * xc)

    _interleave(proj_pieces(z_s, 0, C_YA, D_LRU) + proj_pieces(z_s, D_LRU, C_GATE, D_MODEL),
                _conv_pieces(extb, cbw_ref, cbb_ref, vb_s, S, CONV_B_WIDTH))
    _interleave(proj_pieces(z_s, D_LRU + D_MODEL, C_GATE + D_MODEL, 2 * D_MODEL),
                _scan_pieces(a_s, u_s, hcar, S, chained))

    row_seq = lax.broadcasted_iota(jnp.int32, (SUBLANES * S, HEAD_DIM), 0) % SUBLANES
    for hd in range(N_HEADS):
        c0, c1 = hd * HEAD_DIM, (hd + 1) * HEAD_DIM
        if chained:
            att = _attend(q_s[:, c0:c1], k_ref[0, :, c0:c1], v_ref[0, :, c0:c1])
        else:
            att = jnp.zeros((SUBLANES * S, HEAD_DIM), F32)
            for s in range(SUBLANES):
                att = jnp.where(row_seq == s, _attend(q_s[:, c0:c1], k_ref[s, :, c0:c1], v_ref[s, :, c0:c1]), att)
        att_s[:, c0:c1] = att

    def gate(i):
        return _sigmoid(z_s[:, D_LRU + i * D_MODEL:D_LRU + (i + 1) * D_MODEL] + bg_ref[i:i + 1, :])

    out_a = _dot((u_s[...] * _gelu_tanh(z_s[:, :D_LRU])).astype(BF16), pa_ref[...])
    merged = gate(0) * out_a
    vn = _layer_norm(vb_s[...], lnbg_ref[...], lnbb_ref[...])
    out_b = _dot((vn * _sigmoid(vn)).astype(BF16), pb_ref[...])
    merged = merged + gate(1) * out_b
    out_c = _dot(att_s[...].astype(BF16), pc_ref[...])
    merged = merged + gate(2) * out_c
    y = _dot(merged.astype(BF16), wo_ref[...])
    x1_ref[...] = _layer_norm(alpha * x_ref[...] + y, ln1g_ref[...], ln1b_ref[...])

    @pl.when(t == last_t)
    def _():
        ca_out[...] = hista[...]
        cb_out[...] = histb[...]
        h_out[...] = hcar[...]


def _ffn_kernel(x_ref, wg_ref, wu_ref, wd_ref, g_ref, b_ref, o_ref, *, alpha):
    x = x_ref[...]
    xb = x.astype(BF16)
    gate = _dot(xb, wg_ref[...])
    up = _dot(xb, wu_ref[...])
    hid = (gate * _sigmoid(gate) * up).astype(BF16)
    o_ref[...] = _layer_norm(alpha * x + _dot(hid, wd_ref[...]), g_ref[...], b_ref[...])


def _ln_kernel(x_ref, g_ref, b_ref, o_ref):
    o_ref[...] = _layer_norm(x_ref[...], g_ref[...], b_ref[...])


def _ln_to_segments_kernel(x_ref, g_ref, b_ref, o_ref, *, n_vrows):
    for s in range(SUBLANES):
        o_ref[:, s * D_MODEL:(s + 1) * D_MODEL] = _layer_norm(
            x_ref[s * n_vrows:(s + 1) * n_vrows, :], g_ref[...], b_ref[...])


def _from_segments_kernel(x_ref, o_ref, *, n_vrows):
    for s in range(SUBLANES):
        o_ref[s * n_vrows:(s + 1) * n_vrows, :] = x_ref[:, s * D_MODEL:(s + 1) * D_MODEL]


def _kv_kernel(m_ref, wk_ref, wv_ref, k_ref, v_ref, kb_ref, vb_ref):
    mb = m_ref[...].astype(BF16)
    k = _dot(mb, wk_ref[...])
    v = _dot(mb, wv_ref[...])
    k_ref[...] = k
    v_ref[...] = v
    kb_ref[...] = k.astype(BF16)
    vb_ref[...] = v.astype(BF16)


def _const_spec(shape, layer=None):
    if layer is None:
        return pl.BlockSpec(shape, lambda *_: (0,) * len(shape), pipeline_mode=pl.Buffered(1))
    return pl.BlockSpec((None,) + shape, lambda *_: (layer,) + (0,) * len(shape), pipeline_mode=pl.Buffered(1))


def _row_tile(n_rows, want):
    tile = min(want, n_rows)
    assert n_rows % tile == 0 and tile % SUBLANES == 0
    return tile


def _layer_norm_call(x2d, g, b):
    rows = x2d.shape[0]
    tile = _row_tile(rows, 512)
    return pl.pallas_call(
        _ln_kernel,
        grid=(rows // tile,),
        in_specs=[pl.BlockSpec((tile, D_MODEL), lambda i: (i, 0)),
                  _const_spec((1, D_MODEL)), _const_spec((1, D_MODEL))],
        out_specs=pl.BlockSpec((tile, D_MODEL), lambda i: (i, 0)),
        out_shape=jax.ShapeDtypeStruct(x2d.shape, F32),
        compiler_params=pltpu.CompilerParams(dimension_semantics=("arbitrary",)),
        name="ln_in",
    )(x2d, g.reshape(1, D_MODEL), b.reshape(1, D_MODEL))


def _layer_norm_to_segments_call(x2d, g, b, n_vrows):
    rows = x2d.shape[0]
    tile = SUBLANES * n_vrows
    out = pl.pallas_call(
        functools.partial(_ln_to_segments_kernel, n_vrows=n_vrows),
        grid=(rows // tile,),
        in_specs=[pl.BlockSpec((tile, D_MODEL), lambda i: (i, 0)),
                  _const_spec((1, D_MODEL)), _const_spec((1, D_MODEL))],
        out_specs=pl.BlockSpec((n_vrows, SUBLANES * D_MODEL), lambda i: (i, 0)),
        out_shape=jax.ShapeDtypeStruct((rows // SUBLANES, SUBLANES * D_MODEL), F32),
        compiler_params=pltpu.CompilerParams(dimension_semantics=("arbitrary",)),
        name="ln_in_seg",
    )(x2d, g.reshape(1, D_MODEL), b.reshape(1, D_MODEL))
    return out.reshape(rows, D_MODEL)


def _from_segments_call(x2d, n_vrows):
    rows = x2d.shape[0]
    tile = SUBLANES * n_vrows
    return pl.pallas_call(
        functools.partial(_from_segments_kernel, n_vrows=n_vrows),
        grid=(rows // tile,),
        in_specs=[pl.BlockSpec((n_vrows, SUBLANES * D_MODEL), lambda i: (i, 0))],
        out_specs=pl.BlockSpec((tile, D_MODEL), lambda i: (i, 0)),
        out_shape=jax.ShapeDtypeStruct((rows, D_MODEL), F32),
        compiler_params=pltpu.CompilerParams(dimension_semantics=("arbitrary",)),
        name="from_seg",
    )(x2d.reshape(rows // SUBLANES, SUBLANES * D_MODEL))


def _kv_call(mem, wk, wv):
    depth = wk.shape[0]
    bsz = mem.shape[0]
    out_f32 = jax.ShapeDtypeStruct((depth, bsz, N_MEM, D_XATTN), F32)
    out_bf16 = jax.ShapeDtypeStruct((depth, bsz, N_MEM, D_XATTN), BF16)
    w_spec = pl.BlockSpec((None, D_MODEL, D_XATTN), lambda l, b: (l, 0, 0))
    o_spec = pl.BlockSpec((None, None, N_MEM, D_XATTN), lambda l, b: (l, b, 0, 0))
    return pl.pallas_call(
        _kv_kernel,
        grid=(depth, bsz),
        in_specs=[pl.BlockSpec((None, N_MEM, D_MODEL), lambda l, b: (b, 0, 0)), w_spec, w_spec],
        out_specs=[o_spec, o_spec, o_spec, o_spec],
        out_shape=[out_f32, out_f32, out_bf16, out_bf16],
        compiler_params=pltpu.CompilerParams(dimension_semantics=("arbitrary", "arbitrary")),
        name="mem_kv",
    )(mem, wk, wv)


def _mixer_call(x2d, kb, vb, ca0, h0, cb0, lw, layer, *, chained, n_blocks, n_tiles, n_vrows, alpha, name):
    S = n_vrows
    tile = SUBLANES * S
    n_mem_sets = 1 if chained else SUBLANES
    assert chained or n_tiles == 1
    assert not chained or S >= HIST_B
    kernel = functools.partial(_mixer_kernel, chained=chained, n_vrows=S, alpha=alpha)
    x_spec = pl.BlockSpec((tile, D_MODEL), lambda b, t: (b * n_tiles + t, 0))

    def per_block(shape):
        return pl.BlockSpec((None,) + shape, lambda b, t: (b,) + (0,) * len(shape))

    state_shapes = [(SUBLANES * HIST_A, D_LRU), (SUBLANES, D_LRU), (SUBLANES * HIST_B, D_CONV)]
    in_specs = [
        x_spec,
        pl.BlockSpec((n_mem_sets, N_MEM, D_XATTN), lambda b, t: (b, 0, 0)),
        pl.BlockSpec((n_mem_sets, N_MEM, D_XATTN), lambda b, t: (b, 0, 0)),
        *[per_block(s) for s in state_shapes],
        _const_spec((D_MODEL, lw["w_in"].shape[-1]), layer),
        _const_spec((3, D_MODEL), layer),
        _const_spec((SUBLANES * CONV_A_WIDTH, D_LRU), layer),
        _const_spec((SUBLANES, D_LRU), layer),
        _const_spec((N_LRU_BLOCKS, LRU_BLOCK, 2 * LRU_BLOCK), layer),
        _const_spec((1, D_LRU), layer),
        _const_spec((1, D_LRU), layer),
        _const_spec((1, D_LRU), layer),
        _const_spec((D_LRU, D_MODEL), layer),
        _const_spec((SUBLANES * CONV_B_WIDTH, D_CONV), layer),
        _const_spec((SUBLANES, D_CONV), layer),
        _const_spec((1, D_CONV), layer),
        _const_spec((1, D_CONV), layer),
        _const_spec((D_CONV, D_MODEL), layer),
        _const_spec((D_XATTN, D_MODEL), layer),
        _const_spec((D_MODEL, D_MODEL), layer),
        _const_spec((1, D_MODEL), layer),
        _const_spec((1, D_MODEL), layer),
    ]
    out_specs = [x_spec] + [per_block(s) for s in state_shapes]
    out_shape = [jax.ShapeDtypeStruct(x2d.shape, F32)] + [
        jax.ShapeDtypeStruct((n_blocks,) + s, F32) for s in state_shapes]
    scratch = [
        pltpu.VMEM((SUBLANES * (HIST_A + S), D_LRU), F32),
        pltpu.VMEM((SUBLANES * (HIST_B + S), D_CONV), F32),
        pltpu.VMEM(state_shapes[0], F32),
        pltpu.VMEM(state_shapes[2], F32),
        pltpu.VMEM(state_shapes[1], F32),
        pltpu.VMEM((tile, D_LRU), F32),
        pltpu.VMEM((tile, D_LRU), F32),
        pltpu.VMEM((tile, D_LRU), F32),
        pltpu.VMEM((tile, D_CONV), F32),
        pltpu.VMEM((tile, D_XATTN), F32),
        pltpu.VMEM((tile, D_MODEL), BF16),
        pltpu.VMEM((tile, D_XATTN), BF16),
        pltpu.VMEM((tile, D_LRU + 3 * D_MODEL), F32),
    ]
    return pl.pallas_call(
        kernel,
        grid=(n_blocks, n_tiles),
        in_specs=in_specs,
        out_specs=out_specs,
        out_shape=out_shape,
        scratch_shapes=scratch,
        compiler_params=pltpu.CompilerParams(dimension_semantics=("arbitrary", "arbitrary"),
                                             vmem_limit_bytes=VMEM_LIMIT),
        name=name,
    )(x2d, kb, vb, ca0, h0, cb0,
      lw["w_in"], lw["b_gate"], lw["conv_a_w"], lw["conv_a_b"], lw["w_ri"], lw["lru_b_r"], lw["lru_b_i"],
      lw["lru_lambda"], lw["proj_a"], lw["conv_b_w"], lw["conv_b_b"], lw["ln_b_g"], lw["ln_b_b"],
      lw["proj_b"], lw["proj_c"], lw["w_out"], lw["ln1_g"], lw["ln1_b"])


def _ffn_call(x2d, lw, layer, *, tile, alpha, name):
    rows = x2d.shape[0]
    d_ff = lw["w_ffn_gate"].shape[-1]
    x_spec = pl.BlockSpec((tile, D_MODEL), lambda i: (i, 0))
    return pl.pallas_call(
        functools.partial(_ffn_kernel, alpha=alpha),
        grid=(rows // tile,),
        in_specs=[x_spec,
                  _const_spec((D_MODEL, d_ff), layer), _const_spec((D_MODEL, d_ff), layer),
                  _const_spec((d_ff, D_MODEL), layer),
                  _const_spec((1, D_MODEL), layer), _const_spec((1, D_MODEL), layer)],
        out_specs=x_spec,
        out_shape=jax.ShapeDtypeStruct(x2d.shape, F32),
        compiler_params=pltpu.CompilerParams(dimension_semantics=("arbitrary",),
                                             vmem_limit_bytes=VMEM_LIMIT),
        name=name,
    )(x2d, lw["w_ffn_gate"], lw["w_ffn_up"], lw["w_ffn_down"], lw["ln2_g"], lw["ln2_b"])


def _steps_to_segments(state):
    n_seq, steps, n_ch = state.shape
    return state.transpose(1, 0, 2).reshape(1, steps * n_seq, n_ch)


def _segments_to_steps(state, steps):
    n_ch = state.shape[-1]
    return state.reshape(steps, SUBLANES, n_ch).transpose(1, 0, 2)


def kernel(x_prompt, x_sample, mem_prompt, state_conv_a, state_lru, state_conv_b, cache_mem_k, cache_mem_v,
           ln_in_g, ln_in_b, w_in, b_gate, conv_a_w, conv_a_b, lru_w_r, lru_b_r, lru_w_i, lru_b_i, lru_lambda,
           proj_a, conv_b_w, conv_b_b, ln_b_g, ln_b_b, proj_b, w_mem_k, w_mem_v, proj_c, w_out, ln1_g, ln1_b,
           w_ffn_gate, w_ffn_up, w_ffn_down, ln2_g, ln2_b):
    depth = w_in.shape[0]
    bp, tp, _ = x_prompt.shape
    bs, ts, _ = x_sample.shape
    assert bs == SUBLANES, "the sample group maps its sequences onto the 8 sublanes"
    alpha = (2 * depth) ** 0.25

    row = lambda p: p.reshape(depth, 1, p.shape[-1])
    rep = lambda p: jnp.repeat(p.reshape(depth, -1, p.shape[-1]), SUBLANES, axis=1)
    lw = {
        "w_in": w_in.astype(BF16), "b_gate": b_gate, "conv_a_w": rep(conv_a_w), "conv_a_b": rep(conv_a_b),
        "w_ri": jnp.concatenate([lru_w_r, lru_w_i], axis=-1).astype(BF16),
        "lru_b_r": lru_b_r.reshape(depth, 1, D_LRU), "lru_b_i": lru_b_i.reshape(depth, 1, D_LRU),
        "lru_lambda": row(lru_lambda), "proj_a": proj_a.astype(BF16),
        "conv_b_w": rep(conv_b_w), "conv_b_b": rep(conv_b_b), "ln_b_g": row(ln_b_g), "ln_b_b": row(ln_b_b),
        "proj_b": proj_b.astype(BF16), "proj_c": proj_c.astype(BF16), "w_out": w_out.astype(BF16),
        "ln1_g": row(ln1_g), "ln1_b": row(ln1_b),
        "w_ffn_gate": w_ffn_gate.astype(BF16), "w_ffn_up": w_ffn_up.astype(BF16),
        "w_ffn_down": w_ffn_down.astype(BF16), "ln2_g": row(ln2_g), "ln2_b": row(ln2_b),
    }

    mem_k, mem_v, mem_kb, mem_vb = _kv_call(mem_prompt, w_mem_k.astype(BF16), w_mem_v.astype(BF16))
    cache_kb = cache_mem_k.reshape(depth, bs, N_MEM, D_XATTN).astype(BF16)
    cache_vb = cache_mem_v.reshape(depth, bs, N_MEM, D_XATTN).astype(BF16)

    sp = 32
    assert tp % (SUBLANES * sp) == 0
    n_tiles = tp // (SUBLANES * sp)
    xp = _layer_norm_to_segments_call(x_prompt.reshape(bp * tp, D_MODEL), ln_in_g, ln_in_b, sp)
    xs = _layer_norm_call(x_sample.transpose(1, 0, 2).reshape(ts * bs, D_MODEL), ln_in_g, ln_in_b)

    zero_ca = jnp.zeros((bp, SUBLANES * HIST_A, D_LRU), F32)
    zero_h = jnp.zeros((bp, SUBLANES, D_LRU), F32)
    zero_cb = jnp.zeros((bp, SUBLANES * HIST_B, D_CONV), F32)

    ca_p, h_p, cb_p, ca_s, h_s, cb_s = [], [], [], [], [], []
    for l in range(depth):
        xp, ca, hl, cb = _mixer_call(xp, mem_kb[l], mem_vb[l], zero_ca, zero_h, zero_cb, lw, l, chained=True,
                                     n_blocks=bp, n_tiles=n_tiles, n_vrows=sp, alpha=alpha, name="mixer_prompt")
        xp = _ffn_call(xp, lw, l, tile=_row_tile(bp * tp, 512), alpha=alpha, name="ffn_prompt")
        ca_p.append(ca.reshape(bp, HIST_A, SUBLANES, D_LRU)[:, :, SUBLANES - 1])
        h_p.append(hl[:, SUBLANES - 1])
        cb_p.append(cb.reshape(bp, HIST_B, SUBLANES, D_CONV)[:, :, SUBLANES - 1])
        xs, ca, hl, cb = _mixer_call(xs, cache_kb[l], cache_vb[l], _steps_to_segments(state_conv_a[l]),
                                     state_lru[l][None], _steps_to_segments(state_conv_b[l]), lw, l, chained=False,
                                     n_blocks=1, n_tiles=1, n_vrows=ts, alpha=alpha, name="mixer_sample")
        xs = _ffn_call(xs, lw, l, tile=bs * ts, alpha=alpha, name="ffn_sample")
        ca_s.append(_segments_to_steps(ca[0], HIST_A))
        h_s.append(hl[0])
        cb_s.append(_segments_to_steps(cb[0], HIST_B))

    y_prompt = _from_segments_call(xp, sp).reshape(bp, tp, D_MODEL)
    y_sample = xs.reshape(ts, bs, D_MODEL).transpose(1, 0, 2)
    kv_shape = (depth, bp, N_MEM, N_HEADS, HEAD_DIM)
    return (y_prompt, y_sample,
            jnp.stack(ca_p), jnp.stack(h_p), jnp.stack(cb_p),
            mem_k.reshape(kv_shape), mem_v.reshape(kv_shape),
            jnp.stack(ca_s), jnp.stack(h_s), jnp.stack(cb_s))
```

```python
import functools

import jax
import jax.numpy as jnp
from jax import lax
from jax.experimental import pallas as pl
from jax.experimental.pallas import tpu as pltpu

D_MODEL = 1024
N_MEM = 256
D_LRU = 1024
N_LRU_BLOCKS = 8
LRU_BLOCK = D_LRU // N_LRU_BLOCKS
CONV_A_WIDTH = 4
LRU_C = 8.0
D_CONV = 512
CONV_B_WIDTH = 31
N_HEADS = 4
HEAD_DIM = 128
D_XATTN = N_HEADS * HEAD_DIM
LN_EPS = 1e-5

C_XA = 0
C_YA = D_LRU
C_GLU = 2 * D_LRU
C_Q = 2 * D_LRU + 2 * D_CONV
C_GATE = C_Q + D_XATTN

SUBLANES = 8
LANES = 128
HIST_A = CONV_A_WIDTH - 1
HIST_B = CONV_B_WIDTH - 1
CONV_CHUNK = 8
TAP_BLOCK = 8
MXU_COLS = 256
FFN_SUB_ROWS = 256
VMEM_LIMIT = 52 * 1024 * 1024

BF16 = jnp.bfloat16
F32 = jnp.float32


def _dot(a, b):
    return jnp.dot(a, b, preferred_element_type=F32)


def _sigmoid(x):
    return 0.5 * jnp.tanh(0.5 * x) + 0.5


def _layer_norm(x, g, b):
    mu = jnp.mean(x, axis=-1, keepdims=True)
    xc = x - mu
    var = jnp.mean(xc * xc, axis=-1, keepdims=True)
    return xc * lax.rsqrt(var + LN_EPS) * g + b


def _gelu_tanh(x):
    return x * (0.5 + 0.5 * jnp.tanh(x * (0.7978845608028654 + 0.035677408136300125 * (x * x))))


def _vrow(j, n=1):
    return slice(SUBLANES * j, SUBLANES * (j + n))


def _sublane_index(n_ch):
    return lax.broadcasted_iota(jnp.int32, (SUBLANES, n_ch), 0)


def _fill_halo(ext, hist, n_hist, n_vrows, chained):
    sub = _sublane_index(ext.shape[-1])
    for i in range(n_hist):
        prev = hist[_vrow(i), :]
        if chained:
            cur = ext[_vrow(n_vrows + i), :]
            ext[_vrow(i), :] = pltpu.roll(jnp.where(sub == SUBLANES - 1, prev, cur), 1, 0)
        else:
            ext[_vrow(i), :] = prev
    for i in range(n_hist):
        hist[_vrow(i), :] = ext[_vrow(n_vrows + i), :]


def _conv_pieces(ext, w_ref, b_ref, out_ref, n_vrows, width):
    def piece(g, j0, ch):
        lanes = slice(g * LANES, (g + 1) * LANES)
        accs = [b_ref[:, lanes]] * ch
        for k0 in range(0, width, TAP_BLOCK):
            kn = min(TAP_BLOCK, width - k0)
            rows = [ext[_vrow(j0 + k0 + m), lanes] for m in range(ch + kn - 1)]
            for k in range(kn):
                wk = w_ref[_vrow(k0 + k), lanes]
                accs = [accs[i] + wk * rows[i + k] for i in range(ch)]
        for i in range(ch):
            out_ref[_vrow(j0 + i), lanes] = accs[i]

    return [functools.partial(piece, g, j0, min(CONV_CHUNK, n_vrows - j0))
            for g in range(ext.shape[-1] // LANES) for j0 in range(0, n_vrows, CONV_CHUNK)]


def _permute_rows(perm, x):
    hi = x.astype(BF16)
    rest = x - hi.astype(F32)
    mid = rest.astype(BF16)
    lo = (rest - mid.astype(F32)).astype(BF16)
    return (_dot(perm, hi) + _dot(perm, mid)) + _dot(perm, lo)


def _scan_pieces(a_s, u_s, hcar, n_vrows, chained):
    def piece(g):
        sub = _sublane_index(LANES)
        lanes = slice(g * LANES, (g + 1) * LANES)
        h = hcar[:, lanes]
        if chained:
            h = jnp.where(sub == 0, pltpu.roll(h, 1, 0), 0.0)
        prod = None
        for j in range(n_vrows):
            a = a_s[_vrow(j), lanes]
            h = a * h + u_s[_vrow(j), lanes]
            u_s[_vrow(j), lanes] = h
            if chained:
                prod = a if prod is None else prod * a
                a_s[_vrow(j), lanes] = prod
        if chained:
            seg_a, seg_h = prod, h
            for sft in (1, 2, 4):
                keep = sub >= sft
                a_sh = jnp.where(keep, pltpu.roll(seg_a, sft, 0), 1.0)
                h_sh = jnp.where(keep, pltpu.roll(seg_h, sft, 0), 0.0)
                seg_h = seg_a * h_sh + seg_h
                seg_a = seg_a * a_sh
            h_in = jnp.where(sub >= 1, pltpu.roll(seg_h, 1, 0), 0.0)
            for j in range(n_vrows):
                u_s[_vrow(j), lanes] = u_s[_vrow(j), lanes] + a_s[_vrow(j), lanes] * h_in
            h = seg_h
        hcar[:, lanes] = h

    return [functools.partial(piece, g) for g in range(a_s.shape[-1] // LANES)]


def _interleave(mxu_pieces, vpu_pieces):
    n_m, n_v = len(mxu_pieces), len(vpu_pieces)
    for i, piece in enumerate(mxu_pieces):
        piece()
        for vp in vpu_pieces[i * n_v // n_m:(i + 1) * n_v // n_m]:
            vp()


def _attend(q, k, v):
    sc = lax.dot_general(q, k, (((1,), (1,)), ((), ())), preferred_element_type=F32) * (HEAD_DIM ** -0.5)
    p = jnp.exp(sc - jnp.max(sc, axis=-1, keepdims=True))
    return _dot(p.astype(BF16), v) / jnp.sum(p, axis=-1, keepdims=True)


def _mixer_kernel(x_ref, k_ref, v_ref, ca0_ref, h0_ref, cb0_ref, perm_ref, lng_ref, lnb_ref,
                  w_in_ref, bg_ref, caw_ref, cab_ref, wri_ref, br_ref, bi_ref, lam_ref, pa_ref,
                  cbw_ref, cbb_ref, lnbg_ref, lnbb_ref, pb_ref, pc_ref, wo_ref, ln1g_ref, ln1b_ref,
                  x1_ref, ca_out, h_out, cb_out,
                  exta, extb, hista, histb, hcar, a_s, u_s, xc_s, vb_s, att_s, xb_s, q_s, z_s, xr_s,
                  *, chained, n_vrows, alpha, in_ln, in_perm):
    t = pl.program_id(1)
    last_t = pl.num_programs(1) - 1
    S = n_vrows

    @pl.when(t == 0)
    def _():
        hista[...] = ca0_ref[...]
        histb[...] = cb0_ref[...]
        hcar[...] = h0_ref[...]

    if in_ln or in_perm:
        x_in = x_ref[...]
        if in_ln:
            x_in = _layer_norm(x_in, lng_ref[...], lnb_ref[...])
        if in_perm:
            x_in = _permute_rows(perm_ref[...], x_in)
        xr_s[...] = x_in
        x_res = xr_s
    else:
        x_res = x_ref
    xb_s[...] = x_res[...].astype(BF16)

    def proj(c0, width):
        return _dot(xb_s[...], w_in_ref[:, c0:c0 + width])

    def proj_pieces(dst, dst0, c0, width):
        def piece(off):
            dst[:, dst0 + off:dst0 + off + MXU_COLS] = proj(c0 + off, MXU_COLS).astype(dst.dtype)
        return [functools.partial(piece, off) for off in range(0, width, MXU_COLS)]

    exta[_vrow(HIST_A, S), :] = proj(C_XA, D_LRU)
    glu = proj(C_GLU, 2 * D_CONV)
    extb[_vrow(HIST_B, S), :] = glu[:, :D_CONV] * _sigmoid(glu[:, D_CONV:])
    _fill_halo(exta, hista, HIST_A, S, chained)
    _fill_halo(extb, histb, HIST_B, S, chained)

    _interleave(proj_pieces(q_s, 0, C_Q, D_XATTN), _conv_pieces(exta, caw_ref, cab_ref, xc_s, S, CONV_A_WIDTH))

    lam = lam_ref[...]
    log_sig_lam = jnp.minimum(lam, 0.0) - jnp.log1p(jnp.exp(-jnp.abs(lam)))

    def lru_gate_piece(n):
        c0, c1 = n * LRU_BLOCK, (n + 1) * LRU_BLOCK
        xc = xc_s[:, c0:c1]
        ri = _dot(xc.astype(BF16), wri_ref[n])
        r = _sigmoid(ri[:, :LRU_BLOCK] + br_ref[:, c0:c1])
        gi = _sigmoid(ri[:, LRU_BLOCK:] + bi_ref[:, c0:c1])
        a = jnp.exp((LRU_C * r) * log_sig_lam[:, c0:c1])
        a_s[:, c0:c1] = a
        u_s[:, c0:c1] = jnp.sqrt(1.0 - a * a) * (gi * xc)

    z_pieces = proj_pieces(z_s, 0, C_YA, D_LRU) + proj_pieces(z_s, D_LRU, C_GATE, 3 * D_MODEL)
    n_gate, n_conv = 6, 8
    _interleave(z_pieces[:n_gate], [functools.partial(lru_gate_piece, n) for n in range(N_LRU_BLOCKS)])
    _interleave(z_pieces[n_gate:n_gate + n_conv], _conv_pieces(extb, cbw_ref, cbb_ref, vb_s, S, CONV_B_WIDTH))
    _interleave(z_pieces[n_gate + n_conv:], _scan_pieces(a_s, u_s, hcar, S, chained))

    row_seq = lax.broadcasted_iota(jnp.int32, (SUBLANES * S, HEAD_DIM), 0) % SUBLANES
    for hd in range(N_HEADS):
        c0, c1 = hd * HEAD_DIM, (hd + 1) * HEAD_DIM
        if chained:
            att = _attend(q_s[:, c0:c1], k_ref[0, :, c0:c1], v_ref[0, :, c0:c1])
        else:
            att = jnp.zeros((SUBLANES * S, HEAD_DIM), F32)
            for s in range(SUBLANES):
                att = jnp.where(row_seq == s, _attend(q_s[:, c0:c1], k_ref[s, :, c0:c1], v_ref[s, :, c0:c1]), att)
        att_s[:, c0:c1] = att

    def gate(i):
        return _sigmoid(z_s[:, D_LRU + i * D_MODEL:D_LRU + (i + 1) * D_MODEL] + bg_ref[i:i + 1, :])

    out_a = _dot((u_s[...] * _gelu_tanh(z_s[:, :D_LRU])).astype(BF16), pa_ref[...])
    merged = gate(0) * out_a
    vn = _layer_norm(vb_s[...], lnbg_ref[...], lnbb_ref[...])
    out_b = _dot((vn * _sigmoid(vn)).astype(BF16), pb_ref[...])
    merged = merged + gate(1) * out_b
    out_c = _dot(att_s[...].astype(BF16), pc_ref[...])
    merged = merged + gate(2) * out_c
    y = _dot(merged.astype(BF16), wo_ref[...])
    x1_ref[...] = _layer_norm(alpha * x_res[...] + y, ln1g_ref[...], ln1b_ref[...])

    @pl.when(t == last_t)
    def _():
        ca_out[...] = hista[...]
        cb_out[...] = histb[...]
        h_out[...] = hcar[...]


def _ffn_kernel(x_ref, wg_ref, wu_ref, wd_ref, g_ref, b_ref, perm_ref, o_ref, *, alpha, sub_rows, out_perm):
    for r0 in range(0, x_ref.shape[0], sub_rows):
        rows = slice(r0, r0 + sub_rows)
        x = x_ref[rows, :]
        xb = x.astype(BF16)
        gate = _dot(xb, wg_ref[...])
        up = _dot(xb, wu_ref[...])
        hid = (gate * _sigmoid(gate) * up).astype(BF16)
        y = _layer_norm(alpha * x + _dot(hid, wd_ref[...]), g_ref[...], b_ref[...])
        if out_perm:
            y = _permute_rows(perm_ref[...], y)
        o_ref[rows, :] = y


def _kv_kernel(m_ref, wk_ref, wv_ref, k_ref, v_ref, kb_ref, vb_ref):
    mb = m_ref[...].astype(BF16)
    k = _dot(mb, wk_ref[...])
    v = _dot(mb, wv_ref[...])
    k_ref[...] = k
    v_ref[...] = v
    kb_ref[...] = k.astype(BF16)
    vb_ref[...] = v.astype(BF16)


def _const_spec(shape, layer=None):
    if layer is None:
        return pl.BlockSpec(shape, lambda *_: (0,) * len(shape), pipeline_mode=pl.Buffered(1))
    return pl.BlockSpec((None,) + shape, lambda *_: (layer,) + (0,) * len(shape), pipeline_mode=pl.Buffered(1))


def _row_tile(n_rows, want):
    tile = min(want, n_rows)
    assert n_rows % tile == 0 and tile % SUBLANES == 0
    return tile


def _kv_call(mem, wk, wv):
    depth = wk.shape[0]
    bsz = mem.shape[0]
    out_f32 = jax.ShapeDtypeStruct((depth, bsz, N_MEM, D_XATTN), F32)
    out_bf16 = jax.ShapeDtypeStruct((depth, bsz, N_MEM, D_XATTN), BF16)
    w_spec = pl.BlockSpec((None, D_MODEL, D_XATTN), lambda l, b: (l, 0, 0))
    o_spec = pl.BlockSpec((None, None, N_MEM, D_XATTN), lambda l, b: (l, b, 0, 0))
    return pl.pallas_call(
        _kv_kernel,
        grid=(depth, bsz),
        in_specs=[pl.BlockSpec((None, N_MEM, D_MODEL), lambda l, b: (b, 0, 0)), w_spec, w_spec],
        out_specs=[o_spec, o_spec, o_spec, o_spec],
        out_shape=[out_f32, out_f32, out_bf16, out_bf16],
        compiler_params=pltpu.CompilerParams(dimension_semantics=("arbitrary", "arbitrary")),
        name="mem_kv",
    )(mem, wk, wv)


def _mixer_call(x2d, kb, vb, ca0, h0, cb0, perm, ln_in, lw, layer, *, chained, n_blocks, n_tiles, n_vrows, alpha,
                in_ln, in_perm, name):
    S = n_vrows
    tile = SUBLANES * S
    n_mem_sets = 1 if chained else SUBLANES
    assert chained or n_tiles == 1
    assert not chained or S >= HIST_B
    assert perm.shape == (tile, tile)
    kernel = functools.partial(_mixer_kernel, chained=chained, n_vrows=S, alpha=alpha, in_ln=in_ln, in_perm=in_perm)
    x_spec = pl.BlockSpec((tile, D_MODEL), lambda b, t: (b * n_tiles + t, 0))

    def per_block(shape):
        return pl.BlockSpec((None,) + shape, lambda b, t: (b,) + (0,) * len(shape))

    state_shapes = [(SUBLANES * HIST_A, D_LRU), (SUBLANES, D_LRU), (SUBLANES * HIST_B, D_CONV)]
    in_specs = [
        x_spec,
        pl.BlockSpec((n_mem_sets, N_MEM, D_XATTN), lambda b, t: (b, 0, 0)),
        pl.BlockSpec((n_mem_sets, N_MEM, D_XATTN), lambda b, t: (b, 0, 0)),
        *[per_block(s) for s in state_shapes],
        _const_spec((tile, tile)),
        _const_spec((1, D_MODEL)),
        _const_spec((1, D_MODEL)),
        _const_spec((D_MODEL, lw["w_in"].shape[-1]), layer),
        _const_spec((3, D_MODEL), layer),
        _const_spec((SUBLANES * CONV_A_WIDTH, D_LRU), layer),
        _const_spec((SUBLANES, D_LRU), layer),
        _const_spec((N_LRU_BLOCKS, LRU_BLOCK, 2 * LRU_BLOCK), layer),
        _const_spec((1, D_LRU), layer),
        _const_spec((1, D_LRU), layer),
        _const_spec((1, D_LRU), layer),
        _const_spec((D_LRU, D_MODEL), layer),
        _const_spec((SUBLANES * CONV_B_WIDTH, D_CONV), layer),
        _const_spec((SUBLANES, D_CONV), layer),
        _const_spec((1, D_CONV), layer),
        _const_spec((1, D_CONV), layer),
        _const_spec((D_CONV, D_MODEL), layer),
        _const_spec((D_XATTN, D_MODEL), layer),
        _const_spec((D_MODEL, D_MODEL), layer),
        _const_spec((1, D_MODEL), layer),
        _const_spec((1, D_MODEL), layer),
    ]
    out_specs = [x_spec] + [per_block(s) for s in state_shapes]
    out_shape = [jax.ShapeDtypeStruct(x2d.shape, F32)] + [
        jax.ShapeDtypeStruct((n_blocks,) + s, F32) for s in state_shapes]
    scratch = [
        pltpu.VMEM((SUBLANES * (HIST_A + S), D_LRU), F32),
        pltpu.VMEM((SUBLANES * (HIST_B + S), D_CONV), F32),
        pltpu.VMEM(state_shapes[0], F32),
        pltpu.VMEM(state_shapes[2], F32),
        pltpu.VMEM(state_shapes[1], F32),
        pltpu.VMEM((tile, D_LRU), F32),
        pltpu.VMEM((tile, D_LRU), F32),
        pltpu.VMEM((tile, D_LRU), F32),
        pltpu.VMEM((tile, D_CONV), F32),
        pltpu.VMEM((tile, D_XATTN), F32),
        pltpu.VMEM((tile, D_MODEL), BF16),
        pltpu.VMEM((tile, D_XATTN), BF16),
        pltpu.VMEM((tile, D_LRU + 3 * D_MODEL), F32),
        pltpu.VMEM((tile, D_MODEL) if (in_ln or in_perm) else (SUBLANES, LANES), F32),
    ]
    return pl.pallas_call(
        kernel,
        grid=(n_blocks, n_tiles),
        in_specs=in_specs,
        out_specs=out_specs,
        out_shape=out_shape,
        scratch_shapes=scratch,
        compiler_params=pltpu.CompilerParams(dimension_semantics=("arbitrary", "arbitrary"),
                                             vmem_limit_bytes=VMEM_LIMIT),
        name=name,
    )(x2d, kb, vb, ca0, h0, cb0, perm, ln_in[0], ln_in[1],
      lw["w_in"], lw["b_gate"], lw["conv_a_w"], lw["conv_a_b"], lw["w_ri"], lw["lru_b_r"], lw["lru_b_i"],
      lw["lru_lambda"], lw["proj_a"], lw["conv_b_w"], lw["conv_b_b"], lw["ln_b_g"], lw["ln_b_b"],
      lw["proj_b"], lw["proj_c"], lw["w_out"], lw["ln1_g"], lw["ln1_b"])


def _ffn_call(x2d, perm, lw, layer, *, tile, alpha, out_perm, name):
    rows = x2d.shape[0]
    d_ff = lw["w_ffn_gate"].shape[-1]
    sub_rows = min(FFN_SUB_ROWS, tile)
    assert tile % sub_rows == 0 and (not out_perm or perm.shape == (sub_rows, sub_rows))
    x_spec = pl.BlockSpec((tile, D_MODEL), lambda i: (i, 0))
    return pl.pallas_call(
        functools.partial(_ffn_kernel, alpha=alpha, sub_rows=sub_rows, out_perm=out_perm),
        grid=(rows // tile,),
        in_specs=[x_spec,
                  _const_spec((D_MODEL, d_ff), layer), _const_spec((D_MODEL, d_ff), layer),
                  _const_spec((d_ff, D_MODEL), layer),
                  _const_spec((1, D_MODEL), layer), _const_spec((1, D_MODEL), layer),
                  _const_spec(perm.shape)],
        out_specs=x_spec,
        out_shape=jax.ShapeDtypeStruct(x2d.shape, F32),
        compiler_params=pltpu.CompilerParams(dimension_semantics=("arbitrary",),
                                             vmem_limit_bytes=VMEM_LIMIT),
        name=name,
    )(x2d, lw["w_ffn_gate"], lw["w_ffn_up"], lw["w_ffn_down"], lw["ln2_g"], lw["ln2_b"], perm)


def _steps_to_segments(state):
    n_seq, steps, n_ch = state.shape
    return state.transpose(1, 0, 2).reshape(1, steps * n_seq, n_ch)


def _segments_to_steps(state, steps):
    n_ch = state.shape[-1]
    return state.reshape(steps, SUBLANES, n_ch).transpose(1, 0, 2)


def _segment_permutation(n_vrows):
    tile = SUBLANES * n_vrows
    out_row = jnp.arange(tile)
    src_row = (out_row % SUBLANES) * n_vrows + out_row // SUBLANES
    return (src_row[:, None] == jnp.arange(tile)[None, :]).astype(BF16)


def kernel(x_prompt, x_sample, mem_prompt, state_conv_a, state_lru, state_conv_b, cache_mem_k, cache_mem_v,
           ln_in_g, ln_in_b, w_in, b_gate, conv_a_w, conv_a_b, lru_w_r, lru_b_r, lru_w_i, lru_b_i, lru_lambda,
           proj_a, conv_b_w, conv_b_b, ln_b_g, ln_b_b, proj_b, w_mem_k, w_mem_v, proj_c, w_out, ln1_g, ln1_b,
           w_ffn_gate, w_ffn_up, w_ffn_down, ln2_g, ln2_b):
    depth = w_in.shape[0]
    bp, tp, _ = x_prompt.shape
    bs, ts, _ = x_sample.shape
    assert bs == SUBLANES, "the sample group maps its sequences onto the 8 sublanes"
    alpha = (2 * depth) ** 0.25

    row = lambda p: p.reshape(depth, 1, p.shape[-1])
    rep = lambda p, n: jnp.repeat(p.reshape(depth, -1, p.shape[-1]), n, axis=1)
    lw = {
        "w_in": w_in.astype(BF16), "b_gate": b_gate,
        "conv_a_w": rep(conv_a_w, SUBLANES), "conv_a_b": rep(conv_a_b, SUBLANES),
        "w_ri": jnp.concatenate([lru_w_r, lru_w_i], axis=-1).astype(BF16),
        "lru_b_r": lru_b_r.reshape(depth, 1, D_LRU), "lru_b_i": lru_b_i.reshape(depth, 1, D_LRU),
        "lru_lambda": row(lru_lambda), "proj_a": proj_a.astype(BF16),
        "conv_b_w": rep(conv_b_w, SUBLANES), "conv_b_b": rep(conv_b_b, SUBLANES),
        "ln_b_g": row(ln_b_g), "ln_b_b": row(ln_b_b),
        "proj_b": proj_b.astype(BF16), "proj_c": proj_c.astype(BF16), "w_out": w_out.astype(BF16),
        "ln1_g": row(ln1_g), "ln1_b": row(ln1_b),
        "w_ffn_gate": w_ffn_gate.astype(BF16), "w_ffn_up": w_ffn_up.astype(BF16),
        "w_ffn_down": w_ffn_down.astype(BF16), "ln2_g": row(ln2_g), "ln2_b": row(ln2_b),
    }
    ln_in = (ln_in_g.reshape(1, D_MODEL), ln_in_b.reshape(1, D_MODEL))

    mem_k, mem_v, mem_kb, mem_vb = _kv_call(mem_prompt, w_mem_k.astype(BF16), w_mem_v.astype(BF16))
    cache_kb = cache_mem_k.reshape(depth, bs, N_MEM, D_XATTN).astype(BF16)
    cache_vb = cache_mem_v.reshape(depth, bs, N_MEM, D_XATTN).astype(BF16)

    sp = 32
    assert tp % (SUBLANES * sp) == 0
    n_tiles = tp // (SUBLANES * sp)
    perm_p = _segment_permutation(sp)
    xp = x_prompt.reshape(bp * tp, D_MODEL)
    perm_s = _segment_permutation(ts)
    xs = x_sample.transpose(1, 0, 2).reshape(ts * bs, D_MODEL)

    zero_ca = jnp.zeros((bp, SUBLANES * HIST_A, D_LRU), F32)
    zero_h = jnp.zeros((bp, SUBLANES, D_LRU), F32)
    zero_cb = jnp.zeros((bp, SUBLANES * HIST_B, D_CONV), F32)

    ca_p, h_p, cb_p, ca_s, h_s, cb_s = [], [], [], [], [], []
    for l in range(depth):
        first, last = l == 0, l == depth - 1
        xp, ca, hl, cb = _mixer_call(xp, mem_kb[l], mem_vb[l], zero_ca, zero_h, zero_cb, perm_p, ln_in, lw, l,
                                     chained=True, n_blocks=bp, n_tiles=n_tiles, n_vrows=sp, alpha=alpha,
                                     in_ln=first, in_perm=first, name="mixer_prompt")
        xp = _ffn_call(xp, perm_p.T, lw, l, tile=_row_tile(bp * tp, 1024), alpha=alpha, out_perm=last,
                       name="ffn_prompt")
        ca_p.append(ca.reshape(bp, HIST_A, SUBLANES, D_LRU)[:, :, SUBLANES - 1])
        h_p.append(hl[:, SUBLANES - 1])
        cb_p.append(cb.reshape(bp, HIST_B, SUBLANES, D_CONV)[:, :, SUBLANES - 1])
        xs, ca, hl, cb = _mixer_call(xs, cache_kb[l], cache_vb[l], _steps_to_segments(state_conv_a[l]),
                                     state_lru[l][None], _steps_to_segments(state_conv_b[l]), perm_s, ln_in, lw, l,
                                     chained=False, n_blocks=1, n_tiles=1, n_vrows=ts, alpha=alpha,
                                     in_ln=first, in_perm=False, name="mixer_sample")
        xs = _ffn_call(xs, perm_s, lw, l, tile=bs * ts, alpha=alpha, out_perm=False, name="ffn_sample")
        ca_s.append(_segments_to_steps(ca[0], HIST_A))
        h_s.append(hl[0])
        cb_s.append(_segments_to_steps(cb[0], HIST_B))

    y_prompt = xp.reshape(bp, tp, D_MODEL)
    y_sample = xs.reshape(ts, bs, D_MODEL).transpose(1, 0, 2)
    kv_shape = (depth, bp, N_MEM, N_HEADS, HEAD_DIM)
    return (y_prompt, y_sample,
            jnp.stack(ca_p), jnp.stack(h_p), jnp.stack(cb_p),
            mem_k.reshape(kv_shape), mem_v.reshape(kv_shape),
            jnp.stack(ca_s), jnp.stack(h_s), jnp.stack(cb_s))
```

```python
import functools

import jax
import jax.numpy as jnp
from jax import lax
from jax.experimental import pallas as pl
from jax.experimental.pallas import tpu as pltpu

D_MODEL = 1024
N_MEM = 256
D_LRU = 1024
N_LRU_BLOCKS = 8
LRU_BLOCK = D_LRU // N_LRU_BLOCKS
CONV_A_WIDTH = 4
LRU_C = 8.0
D_CONV = 512
CONV_B_WIDTH = 31
N_HEADS = 4
HEAD_DIM = 128
D_XATTN = N_HEADS * HEAD_DIM
LN_EPS = 1e-5
LOG2_E = 1.4426950408889634

C_XA = 0
C_YA = D_LRU
C_GLU = 2 * D_LRU
C_Q = 2 * D_LRU + 2 * D_CONV
C_GATE = C_Q + D_XATTN

SUBLANES = 8
LANES = 128
HIST_A = CONV_A_WIDTH - 1
HIST_B = CONV_B_WIDTH - 1
CONV_CHUNK = 8
TAP_BLOCK = 8
MXU_COLS = 256
FFN_SUB_ROWS = 256
TAIL_CHUNKS = 1
LN_CHUNKS = 4
VMEM_LIMIT = 52 * 1024 * 1024

BF16 = jnp.bfloat16
F32 = jnp.float32


def _dot(a, b):
    return jnp.dot(a, b, preferred_element_type=F32)


def _twice_sigmoid_of_twice(x):
    return jnp.tanh(x) + 1.0


def _layer_norm(x, g, b):
    mu = jnp.mean(x, axis=-1, keepdims=True)
    xc = x - mu
    var = jnp.mean(xc * xc, axis=-1, keepdims=True)
    return xc * lax.rsqrt(var + LN_EPS) * g + b


def _twice_gelu_tanh(x):
    return x * (1.0 + jnp.tanh(x * (0.7978845608028654 + 0.035677408136300125 * (x * x))))


def _vrow(j, n=1):
    return slice(SUBLANES * j, SUBLANES * (j + n))


def _sublane_index(n_ch):
    return lax.broadcasted_iota(jnp.int32, (SUBLANES, n_ch), 0)


def _fill_halo(ext, hist, n_hist, n_vrows, chained):
    sub = _sublane_index(ext.shape[-1])
    for i in range(n_hist):
        prev = hist[_vrow(i), :]
        if chained:
            cur = ext[_vrow(n_vrows + i), :]
            ext[_vrow(i), :] = pltpu.roll(jnp.where(sub == SUBLANES - 1, prev, cur), 1, 0)
        else:
            ext[_vrow(i), :] = prev
    for i in range(n_hist):
        hist[_vrow(i), :] = ext[_vrow(n_vrows + i), :]


def _conv_pieces(ext, w_ref, b_ref, out_ref, n_vrows, width):
    def piece(g, j0, ch):
        lanes = slice(g * LANES, (g + 1) * LANES)
        accs = [b_ref[:, lanes]] * ch
        for k0 in range(0, width, TAP_BLOCK):
            kn = min(TAP_BLOCK, width - k0)
            rows = [ext[_vrow(j0 + k0 + m), lanes] for m in range(ch + kn - 1)]
            for k in range(kn):
                wk = w_ref[_vrow(k0 + k), lanes]
                accs = [accs[i] + wk * rows[i + k] for i in range(ch)]
        for i in range(ch):
            out_ref[_vrow(j0 + i), lanes] = accs[i]

    return [functools.partial(piece, g, j0, min(CONV_CHUNK, n_vrows - j0))
            for g in range(ext.shape[-1] // LANES) for j0 in range(0, n_vrows, CONV_CHUNK)]


def _permute_rows(perm, x):
    hi = x.astype(BF16)
    rest = x - hi.astype(F32)
    mid = rest.astype(BF16)
    lo = (rest - mid.astype(F32)).astype(BF16)
    return (_dot(perm, hi) + _dot(perm, mid)) + _dot(perm, lo)


def _scan_pieces(a_s, u_s, hcar, n_vrows, chained):
    def piece(g):
        sub = _sublane_index(LANES)
        lanes = slice(g * LANES, (g + 1) * LANES)
        h = hcar[:, lanes]
        if chained:
            h = jnp.where(sub == 0, pltpu.roll(h, 1, 0), 0.0)
        prod = None
        for j in range(n_vrows):
            a = a_s[_vrow(j), lanes]
            h = a * h + u_s[_vrow(j), lanes]
            u_s[_vrow(j), lanes] = h
            if chained:
                prod = a if prod is None else prod * a
                a_s[_vrow(j), lanes] = prod
        if chained:
            seg_a, seg_h = prod, h
            for sft in (1, 2, 4):
                keep = sub >= sft
                a_sh = jnp.where(keep, pltpu.roll(seg_a, sft, 0), 1.0)
                h_sh = jnp.where(keep, pltpu.roll(seg_h, sft, 0), 0.0)
                seg_h = seg_a * h_sh + seg_h
                seg_a = seg_a * a_sh
            h_in = jnp.where(sub >= 1, pltpu.roll(seg_h, 1, 0), 0.0)
            for j in range(n_vrows):
                u_s[_vrow(j), lanes] = u_s[_vrow(j), lanes] + a_s[_vrow(j), lanes] * h_in
            h = seg_h
        hcar[:, lanes] = h

    return [functools.partial(piece, g) for g in range(a_s.shape[-1] // LANES)]


def _interleave(mxu_pieces, vpu_pieces):
    n_m, n_v = len(mxu_pieces), len(vpu_pieces)
    for i, piece in enumerate(mxu_pieces):
        piece()
        for vp in vpu_pieces[i * n_v // n_m:(i + 1) * n_v // n_m]:
            vp()


def _attend(q, k, v):
    sc = lax.dot_general(q, k, (((1,), (1,)), ((), ())), preferred_element_type=F32)
    p = jnp.exp2((sc - jnp.max(sc, axis=-1, keepdims=True)) * (HEAD_DIM ** -0.5 * LOG2_E))
    return _dot(p.astype(BF16), v) / jnp.sum(p, axis=-1, keepdims=True)


def _mixer_kernel(x_ref, k_ref, v_ref, ca0_ref, h0_ref, cb0_ref, perm_ref, lng_ref, lnb_ref,
                  w_in_ref, bg_ref, caw_ref, cab_ref, wri_ref, br_ref, bi_ref, lam_ref, pa_ref,
                  cbw_ref, cbb_ref, lnbg_ref, lnbb_ref, pb_ref, pc_ref, wo_ref, ln1g_ref, ln1b_ref,
                  x1_ref, ca_out, h_out, cb_out,
                  exta, extb, hista, histb, hcar, a_s, u_s, xc_s, vb_s, att_s, xb_s, q_s, z_s, xr_s, ypre_s,
                  *, chained, n_vrows, n_tiles, alpha, in_ln, in_perm):
    step = pl.program_id(0)
    n_real = pl.num_programs(0) - 1
    t = step % n_tiles
    S = n_vrows

    @pl.when(step == 0)
    def _():
        ypre_s[...] = jnp.zeros(ypre_s.shape, F32)

    def prev_ln_pieces():
        chunk = (SUBLANES * S) // LN_CHUNKS

        def piece(c):
            rows = slice(c * chunk, (c + 1) * chunk)
            x1_ref[rows, :] = _layer_norm(ypre_s[rows, :], ln1g_ref[...], ln1b_ref[...])
        return [functools.partial(piece, c) for c in range(LN_CHUNKS)]

    @pl.when(step == n_real)
    def _():
        for piece in prev_ln_pieces():
            piece()

    @pl.when(step < n_real)
    def _():
        @pl.when(t == 0)
        def _():
            hista[...] = ca0_ref[...]
            histb[...] = cb0_ref[...]
            hcar[...] = h0_ref[...]

        if in_ln or in_perm:
            x_in = x_ref[...]
            if in_ln:
                x_in = _layer_norm(x_in, lng_ref[...], lnb_ref[...])
            if in_perm:
                x_in = _permute_rows(perm_ref[...], x_in)
            xr_s[...] = x_in
            x_res = xr_s
        else:
            x_res = x_ref
        xb_s[...] = x_res[...].astype(BF16)

        def proj(c0, width):
            return _dot(xb_s[...], w_in_ref[:, c0:c0 + width])

        def proj_pieces(dst, dst0, c0, width):
            def piece(off):
                dst[:, dst0 + off:dst0 + off + MXU_COLS] = proj(c0 + off, MXU_COLS).astype(dst.dtype)
            return [functools.partial(piece, off) for off in range(0, width, MXU_COLS)]

        _interleave(proj_pieces(exta.at[_vrow(HIST_A, S), :], 0, C_XA, D_LRU), prev_ln_pieces())
        glu = proj(C_GLU, 2 * D_CONV)
        extb[_vrow(HIST_B, S), :] = glu[:, :D_CONV] * _twice_sigmoid_of_twice(glu[:, D_CONV:])
        _fill_halo(exta, hista, HIST_A, S, chained)
        _fill_halo(extb, histb, HIST_B, S, chained)

        _interleave(proj_pieces(q_s, 0, C_Q, D_XATTN), _conv_pieces(exta, caw_ref, cab_ref, xc_s, S, CONV_A_WIDTH))

        lam = lam_ref[...]
        log_sig_lam = jnp.minimum(lam, 0.0) - jnp.log1p(jnp.exp(-jnp.abs(lam)))
        decay = (0.5 * LRU_C * LOG2_E) * log_sig_lam

        def lru_gate_piece(n):
            c0, c1 = n * LRU_BLOCK, (n + 1) * LRU_BLOCK
            xc_half = xc_s[:, c0:c1]
            ri = _dot(xc_half.astype(BF16), wri_ref[n])
            r2 = _twice_sigmoid_of_twice(ri[:, :LRU_BLOCK] + br_ref[:, c0:c1])
            i2 = _twice_sigmoid_of_twice(ri[:, LRU_BLOCK:] + bi_ref[:, c0:c1])
            a = jnp.exp2(r2 * decay[:, c0:c1])
            a_s[:, c0:c1] = a
            u_s[:, c0:c1] = jnp.sqrt(1.0 - a * a) * (i2 * xc_half)

        z_pieces = proj_pieces(z_s, 0, C_YA, D_LRU) + proj_pieces(z_s, D_LRU, C_GATE, 3 * D_MODEL)
        n_gate, n_conv = 6, 8
        _interleave(z_pieces[:n_gate], [functools.partial(lru_gate_piece, n) for n in range(N_LRU_BLOCKS)])
        _interleave(z_pieces[n_gate:n_gate + n_conv], _conv_pieces(extb, cbw_ref, cbb_ref, vb_s, S, CONV_B_WIDTH))
        _interleave(z_pieces[n_gate + n_conv:], _scan_pieces(a_s, u_s, hcar, S, chained))

        row_seq = lax.broadcasted_iota(jnp.int32, (SUBLANES * S, HEAD_DIM), 0) % SUBLANES
        for hd in range(N_HEADS):
            c0, c1 = hd * HEAD_DIM, (hd + 1) * HEAD_DIM
            if chained:
                att = _attend(q_s[:, c0:c1], k_ref[0, :, c0:c1], v_ref[0, :, c0:c1])
            else:
                att = jnp.zeros((SUBLANES * S, HEAD_DIM), F32)
                for s in range(SUBLANES):
                    att = jnp.where(row_seq == s, _attend(q_s[:, c0:c1], k_ref[s, :, c0:c1], v_ref[s, :, c0:c1]), att)
            att_s[:, c0:c1] = att

        chunk = (SUBLANES * S) // TAIL_CHUNKS
        for c in range(TAIL_CHUNKS):
            rows = slice(c * chunk, (c + 1) * chunk)

            def gate2(i):
                cols = slice(D_LRU + i * D_MODEL, D_LRU + (i + 1) * D_MODEL)
                return _twice_sigmoid_of_twice(z_s[rows, cols] + bg_ref[i:i + 1, :])

            out_a = _dot((u_s[rows, :] * _twice_gelu_tanh(z_s[rows, :D_LRU])).astype(BF16), pa_ref[...])
            merged2 = gate2(0) * out_a
            vn_half = _layer_norm(vb_s[rows, :], lnbg_ref[...], lnbb_ref[...])
            out_b = _dot((vn_half * _twice_sigmoid_of_twice(vn_half)).astype(BF16), pb_ref[...])
            merged2 = merged2 + gate2(1) * out_b
            out_c = _dot(att_s[rows, :].astype(BF16), pc_ref[...])
            merged2 = merged2 + gate2(2) * out_c
            y = _dot(merged2.astype(BF16), wo_ref[...])
            ypre_s[rows, :] = alpha * x_res[rows, :] + y

        @pl.when(t == n_tiles - 1)
        def _():
            ca_out[...] = hista[...]
            cb_out[...] = histb[...]
            h_out[...] = hcar[...]


def _ffn_kernel(x_ref, wg_ref, wu_ref, wd_ref, g_ref, b_ref, perm_ref, o_ref, *, alpha, sub_rows, out_perm):
    for r0 in range(0, x_ref.shape[0], sub_rows):
        rows = slice(r0, r0 + sub_rows)
        x = x_ref[rows, :]
        xb = x.astype(BF16)
        gate_half = _dot(xb, wg_ref[...])
        up = _dot(xb, wu_ref[...])
        hid = (gate_half * _twice_sigmoid_of_twice(gate_half) * up).astype(BF16)
        y = _layer_norm(alpha * x + _dot(hid, wd_ref[...]), g_ref[...], b_ref[...])
        if out_perm:
            y = _permute_rows(perm_ref[...], y)
        o_ref[rows, :] = y


def _kv_kernel(m_ref, wk_ref, wv_ref, k_ref, v_ref, kb_ref, vb_ref):
    mb = m_ref[...].astype(BF16)
    k = _dot(mb, wk_ref[...])
    v = _dot(mb, wv_ref[...])
    k_ref[...] = k
    v_ref[...] = v
    kb_ref[...] = k.astype(BF16)
    vb_ref[...] = v.astype(BF16)


def _const_spec(shape, layer=None):
    if layer is None:
        return pl.BlockSpec(shape, lambda *_: (0,) * len(shape), pipeline_mode=pl.Buffered(1))
    return pl.BlockSpec((None,) + shape, lambda *_: (layer,) + (0,) * len(shape), pipeline_mode=pl.Buffered(1))


def _row_tile(n_rows, want):
    tile = min(want, n_rows)
    assert n_rows % tile == 0 and tile % SUBLANES == 0
    return tile


def _kv_call(mem, wk, wv):
    depth = wk.shape[0]
    bsz = mem.shape[0]
    out_f32 = jax.ShapeDtypeStruct((depth, bsz, N_MEM, D_XATTN), F32)
    out_bf16 = jax.ShapeDtypeStruct((depth, bsz, N_MEM, D_XATTN), BF16)
    w_spec = pl.BlockSpec((None, D_MODEL, D_XATTN), lambda l, b: (l, 0, 0))
    o_spec = pl.BlockSpec((None, None, N_MEM, D_XATTN), lambda l, b: (l, b, 0, 0))
    return pl.pallas_call(
        _kv_kernel,
        grid=(depth, bsz),
        in_specs=[pl.BlockSpec((None, N_MEM, D_MODEL), lambda l, b: (b, 0, 0)), w_spec, w_spec],
        out_specs=[o_spec, o_spec, o_spec, o_spec],
        out_shape=[out_f32, out_f32, out_bf16, out_bf16],
        compiler_params=pltpu.CompilerParams(dimension_semantics=("arbitrary", "arbitrary")),
        name="mem_kv",
    )(mem, wk, wv)


def _mixer_call(x2d, kb, vb, ca0, h0, cb0, perm, ln_in, lw, layer, *, chained, n_blocks, n_tiles, n_vrows, alpha,
                in_ln, in_perm, name):
    S = n_vrows
    tile = SUBLANES * S
    n_mem_sets = 1 if chained else SUBLANES
    assert chained or n_tiles == 1
    assert not chained or S >= HIST_B
    assert perm.shape == (tile, tile)
    kernel = functools.partial(_mixer_kernel, chained=chained, n_vrows=S, n_tiles=n_tiles, alpha=alpha,
                               in_ln=in_ln, in_perm=in_perm)
    n_real = n_blocks * n_tiles
    x_spec = pl.BlockSpec((tile, D_MODEL), lambda i: (jnp.minimum(i, n_real - 1), 0))
    o_spec = pl.BlockSpec((tile, D_MODEL), lambda i: (jnp.maximum(i - 1, 0), 0))

    def block_of(i):
        return jnp.minimum(i, n_real - 1) // n_tiles

    def per_block(shape):
        return pl.BlockSpec((None,) + shape, lambda i: (block_of(i),) + (0,) * len(shape))

    state_shapes = [(SUBLANES * HIST_A, D_LRU), (SUBLANES, D_LRU), (SUBLANES * HIST_B, D_CONV)]
    in_specs = [
        x_spec,
        pl.BlockSpec((n_mem_sets, N_MEM, D_XATTN), lambda i: (block_of(i), 0, 0)),
        pl.BlockSpec((n_mem_sets, N_MEM, D_XATTN), lambda i: (block_of(i), 0, 0)),
        *[per_block(s) for s in state_shapes],
        _const_spec((tile, tile)),
        _const_spec((1, D_MODEL)),
        _const_spec((1, D_MODEL)),
        _const_spec((D_MODEL, lw["w_in"].shape[-1]), layer),
        _const_spec((3, D_MODEL), layer),
        _const_spec((SUBLANES * CONV_A_WIDTH, D_LRU), layer),
        _const_spec((SUBLANES, D_LRU), layer),
        _const_spec((N_LRU_BLOCKS, LRU_BLOCK, 2 * LRU_BLOCK), layer),
        _const_spec((1, D_LRU), layer),
        _const_spec((1, D_LRU), layer),
        _const_spec((1, D_LRU), layer),
        _const_spec((D_LRU, D_MODEL), layer),
        _const_spec((SUBLANES * CONV_B_WIDTH, D_CONV), layer),
        _const_spec((SUBLANES, D_CONV), layer),
        _const_spec((1, D_CONV), layer),
        _const_spec((1, D_CONV), layer),
        _const_spec((D_CONV, D_MODEL), layer),
        _const_spec((D_XATTN, D_MODEL), layer),
        _const_spec((D_MODEL, D_MODEL), layer),
        _const_spec((1, D_MODEL), layer),
        _const_spec((1, D_MODEL), layer),
    ]
    out_specs = [o_spec] + [per_block(s) for s in state_shapes]
    out_shape = [jax.ShapeDtypeStruct(x2d.shape, F32)] + [
        jax.ShapeDtypeStruct((n_blocks,) + s, F32) for s in state_shapes]
    scratch = [
        pltpu.VMEM((SUBLANES * (HIST_A + S), D_LRU), F32),
        pltpu.VMEM((SUBLANES * (HIST_B + S), D_CONV), F32),
        pltpu.VMEM(state_shapes[0], F32),
        pltpu.VMEM(state_shapes[2], F32),
        pltpu.VMEM(state_shapes[1], F32),
        pltpu.VMEM((tile, D_LRU), F32),
        pltpu.VMEM((tile, D_LRU), F32),
        pltpu.VMEM((tile, D_LRU), F32),
        pltpu.VMEM((tile, D_CONV), F32),
        pltpu.VMEM((tile, D_XATTN), F32),
        pltpu.VMEM((tile, D_MODEL), BF16),
        pltpu.VMEM((tile, D_XATTN), BF16),
        pltpu.VMEM((tile, D_LRU + 3 * D_MODEL), F32),
        pltpu.VMEM((tile, D_MODEL) if (in_ln or in_perm) else (SUBLANES, LANES), F32),
        pltpu.VMEM((tile, D_MODEL), F32),
    ]
    return pl.pallas_call(
        kernel,
        grid=(n_real + 1,),
        in_specs=in_specs,
        out_specs=out_specs,
        out_shape=out_shape,
        scratch_shapes=scratch,
        compiler_params=pltpu.CompilerParams(dimension_semantics=("arbitrary",),
                                             vmem_limit_bytes=VMEM_LIMIT),
        name=name,
    )(x2d, kb, vb, ca0, h0, cb0, perm, ln_in[0], ln_in[1],
      lw["w_in"], lw["b_gate"], lw["conv_a_w"], lw["conv_a_b"], lw["w_ri"], lw["lru_b_r"], lw["lru_b_i"],
      lw["lru_lambda"], lw["proj_a"], lw["conv_b_w"], lw["conv_b_b"], lw["ln_b_g"], lw["ln_b_b"],
      lw["proj_b"], lw["proj_c"], lw["w_out"], lw["ln1_g"], lw["ln1_b"])


def _ffn_call(x2d, perm, lw, layer, *, tile, alpha, out_perm, name):
    rows = x2d.shape[0]
    d_ff = lw["w_ffn_gate"].shape[-1]
    sub_rows = min(FFN_SUB_ROWS, tile)
    assert tile % sub_rows == 0 and (not out_perm or perm.shape == (sub_rows, sub_rows))
    x_spec = pl.BlockSpec((tile, D_MODEL), lambda i: (i, 0))
    return pl.pallas_call(
        functools.partial(_ffn_kernel, alpha=alpha, sub_rows=sub_rows, out_perm=out_perm),
        grid=(rows // tile,),
        in_specs=[x_spec,
                  _const_spec((D_MODEL, d_ff), layer), _const_spec((D_MODEL, d_ff), layer),
                  _const_spec((d_ff, D_MODEL), layer),
                  _const_spec((1, D_MODEL), layer), _const_spec((1, D_MODEL), layer),
                  _const_spec(perm.shape)],
        out_specs=x_spec,
        out_shape=jax.ShapeDtypeStruct(x2d.shape, F32),
        compiler_params=pltpu.CompilerParams(dimension_semantics=("arbitrary",),
                                             vmem_limit_bytes=VMEM_LIMIT),
        name=name,
    )(x2d, lw["w_ffn_gate"], lw["w_ffn_up"], lw["w_ffn_down"], lw["ln2_g"], lw["ln2_b"], perm)


def _steps_to_segments(state):
    n_seq, steps, n_ch = state.shape
    return state.transpose(1, 0, 2).reshape(1, steps * n_seq, n_ch)


def _segments_to_steps(state, steps):
    n_ch = state.shape[-1]
    return state.reshape(steps, SUBLANES, n_ch).transpose(1, 0, 2)


def _segment_permutation(n_vrows):
    tile = SUBLANES * n_vrows
    out_row = jnp.arange(tile)
    src_row = (out_row % SUBLANES) * n_vrows + out_row // SUBLANES
    return (src_row[:, None] == jnp.arange(tile)[None, :]).astype(BF16)


def kernel(x_prompt, x_sample, mem_prompt, state_conv_a, state_lru, state_conv_b, cache_mem_k, cache_mem_v,
           ln_in_g, ln_in_b, w_in, b_gate, conv_a_w, conv_a_b, lru_w_r, lru_b_r, lru_w_i, lru_b_i, lru_lambda,
           proj_a, conv_b_w, conv_b_b, ln_b_g, ln_b_b, proj_b, w_mem_k, w_mem_v, proj_c, w_out, ln1_g, ln1_b,
           w_ffn_gate, w_ffn_up, w_ffn_down, ln2_g, ln2_b):
    depth = w_in.shape[0]
    bp, tp, _ = x_prompt.shape
    bs, ts, _ = x_sample.shape
    assert bs == SUBLANES, "the sample group maps its sequences onto the 8 sublanes"
    alpha = (2 * depth) ** 0.25

    row = lambda p: p.reshape(depth, 1, p.shape[-1])
    rep = lambda p, n: jnp.repeat(p.reshape(depth, -1, p.shape[-1]), n, axis=1)
    half_cols = jnp.concatenate([jnp.full((C_GLU,), 1.0, F32), jnp.full((2 * D_CONV,), 0.5, F32),
                                 jnp.full((D_XATTN,), 1.0, F32), jnp.full((3 * D_MODEL,), 0.5, F32)])
    lw = {
        "w_in": (w_in * half_cols).astype(BF16), "b_gate": 0.5 * b_gate,
        "conv_a_w": rep(0.5 * conv_a_w, SUBLANES), "conv_a_b": rep(0.5 * conv_a_b, SUBLANES),
        "w_ri": jnp.concatenate([lru_w_r, lru_w_i], axis=-1).astype(BF16),
        "lru_b_r": 0.5 * lru_b_r.reshape(depth, 1, D_LRU), "lru_b_i": 0.5 * lru_b_i.reshape(depth, 1, D_LRU),
        "lru_lambda": row(lru_lambda), "proj_a": (0.5 * proj_a).astype(BF16),
        "conv_b_w": rep(conv_b_w, SUBLANES), "conv_b_b": rep(conv_b_b, SUBLANES),
        "ln_b_g": 0.5 * row(ln_b_g), "ln_b_b": 0.5 * row(ln_b_b),
        "proj_b": proj_b.astype(BF16), "proj_c": proj_c.astype(BF16), "w_out": (0.5 * w_out).astype(BF16),
        "ln1_g": row(ln1_g), "ln1_b": row(ln1_b),
        "w_ffn_gate": (0.5 * w_ffn_gate).astype(BF16), "w_ffn_up": w_ffn_up.astype(BF16),
        "w_ffn_down": w_ffn_down.astype(BF16), "ln2_g": row(ln2_g), "ln2_b": row(ln2_b),
    }
    ln_in = (ln_in_g.reshape(1, D_MODEL), ln_in_b.reshape(1, D_MODEL))

    mem_k, mem_v, mem_kb, mem_vb = _kv_call(mem_prompt, w_mem_k.astype(BF16), w_mem_v.astype(BF16))
    cache_kb = cache_mem_k.reshape(depth, bs, N_MEM, D_XATTN).astype(BF16)
    cache_vb = cache_mem_v.reshape(depth, bs, N_MEM, D_XATTN).astype(BF16)

    sp = 32
    assert tp % (SUBLANES * sp) == 0
    n_tiles = tp // (SUBLANES * sp)
    perm_p = _segment_permutation(sp)
    xp = x_prompt.reshape(bp * tp, D_MODEL)
    perm_s = _segment_permutation(ts)
    xs = x_sample.transpose(1, 0, 2).reshape(ts * bs, D_MODEL)

    zero_ca = jnp.zeros((bp, SUBLANES * HIST_A, D_LRU), F32)
    zero_h = jnp.zeros((bp, SUBLANES, D_LRU), F32)
    zero_cb = jnp.zeros((bp, SUBLANES * HIST_B, D_CONV), F32)

    ca_p, h_p, cb_p, ca_s, h_s, cb_s = [], [], [], [], [], []
    for l in range(depth):
        first, last = l == 0, l == depth - 1
        xp, ca, hl, cb = _mixer_call(xp, mem_kb[l], mem_vb[l], zero_ca, zero_h, zero_cb, perm_p, ln_in, lw, l,
                                     chained=True, n_blocks=bp, n_tiles=n_tiles, n_vrows=sp, alpha=alpha,
                                     in_ln=first, in_perm=first, name="mixer_prompt")
        xp = _ffn_call(xp, perm_p.T, lw, l, tile=_row_tile(bp * tp, 1024), alpha=alpha, out_perm=last,
                       name="ffn_prompt")
        ca_p.append(ca.reshape(bp, HIST_A, SUBLANES, D_LRU)[:, :, SUBLANES - 1])
        h_p.append(hl[:, SUBLANES - 1])
        cb_p.append(cb.reshape(bp, HIST_B, SUBLANES, D_CONV)[:, :, SUBLANES - 1])
        xs, ca, hl, cb = _mixer_call(xs, cache_kb[l], cache_vb[l], _steps_to_segments(state_conv_a[l]),
                                     state_lru[l][None], _steps_to_segments(state_conv_b[l]), perm_s, ln_in, lw, l,
                                     chained=False, n_blocks=1, n_tiles=1, n_vrows=ts, alpha=alpha,
                                     in_ln=first, in_perm=False, name="mixer_sample")
        xs = _ffn_call(xs, perm_s, lw, l, tile=bs * ts, alpha=alpha, out_perm=False, name="ffn_sample")
        ca_s.append(_segments_to_steps(ca[0], HIST_A))
        h_s.append(hl[0])
        cb_s.append(_segments_to_steps(cb[0], HIST_B))

    y_prompt = xp.reshape(bp, tp, D_MODEL)
    y_sample = xs.reshape(ts, bs, D_MODEL).transpose(1, 0, 2)
    kv_shape = (depth, bp, N_MEM, N_HEADS, HEAD_DIM)
    return (y_prompt, y_sample,
            jnp.stack(ca_p), jnp.stack(h_p), jnp.stack(cb_p),
            mem_k.reshape(kv_shape), mem_v.reshape(kv_shape),
            jnp.stack(ca_s), jnp.stack(h_s), jnp.stack(cb_s))
```

```python
import functools

import jax
import jax.numpy as jnp
from jax import lax
from jax.experimental import pallas as pl
from jax.experimental.pallas import tpu as pltpu

D_MODEL = 1024
N_MEM = 256
D_LRU = 1024
N_LRU_BLOCKS = 8
LRU_BLOCK = D_LRU // N_LRU_BLOCKS
CONV_A_WIDTH = 4
LRU_C = 8.0
D_CONV = 512
CONV_B_WIDTH = 31
N_HEADS = 4
HEAD_DIM = 128
D_XATTN = N_HEADS * HEAD_DIM
LN_EPS = 1e-5
LOG2_E = 1.4426950408889634

C_XA = 0
C_YA = D_LRU
C_GLU = 2 * D_LRU
C_Q = 2 * D_LRU + 2 * D_CONV
C_GATE = C_Q + D_XATTN

SUBLANES = 8
LANES = 128
HIST_A = CONV_A_WIDTH - 1
HIST_B = CONV_B_WIDTH - 1
CONV_CHUNK = 8
TAP_BLOCK = 8
MXU_COLS = 256
FFN_SUB_ROWS = 256
TAIL_CHUNKS = 1
LN_CHUNKS = 4
VMEM_LIMIT = 52 * 1024 * 1024

BF16 = jnp.bfloat16
F32 = jnp.float32


def _dot(a, b):
    return jnp.dot(a, b, preferred_element_type=F32)


def _twice_sigmoid_of_twice(x):
    return jnp.tanh(x) + 1.0


def _layer_norm(x, g, b):
    mu = jnp.mean(x, axis=-1, keepdims=True)
    xc = x - mu
    var = jnp.mean(xc * xc, axis=-1, keepdims=True)
    return xc * lax.rsqrt(var + LN_EPS) * g + b


def _twice_gelu_tanh(x):
    return x * (1.0 + jnp.tanh(x * (0.7978845608028654 + 0.035677408136300125 * (x * x))))


def _vrow(j, n=1):
    return slice(SUBLANES * j, SUBLANES * (j + n))


def _sublane_index(n_ch):
    return lax.broadcasted_iota(jnp.int32, (SUBLANES, n_ch), 0)


def _fill_halo(ext, hist, n_hist, n_vrows, chained):
    sub = _sublane_index(ext.shape[-1])
    for i in range(n_hist):
        prev = hist[_vrow(i), :]
        if chained:
            cur = ext[_vrow(n_vrows + i), :]
            ext[_vrow(i), :] = pltpu.roll(jnp.where(sub == SUBLANES - 1, prev, cur), 1, 0)
        else:
            ext[_vrow(i), :] = prev
    for i in range(n_hist):
        hist[_vrow(i), :] = ext[_vrow(n_vrows + i), :]


def _conv_pieces(ext, w_ref, b_ref, out_ref, n_vrows, width):
    def piece(g, j0, ch):
        lanes = slice(g * LANES, (g + 1) * LANES)
        accs = [b_ref[:, lanes]] * ch
        for k0 in range(0, width, TAP_BLOCK):
            kn = min(TAP_BLOCK, width - k0)
            rows = [ext[_vrow(j0 + k0 + m), lanes] for m in range(ch + kn - 1)]
            for k in range(kn):
                wk = w_ref[_vrow(k0 + k), lanes]
                accs = [accs[i] + wk * rows[i + k] for i in range(ch)]
        for i in range(ch):
            out_ref[_vrow(j0 + i), lanes] = accs[i]

    return [functools.partial(piece, g, j0, min(CONV_CHUNK, n_vrows - j0))
            for g in range(ext.shape[-1] // LANES) for j0 in range(0, n_vrows, CONV_CHUNK)]


def _to_segment_major(x, slab, dst, n_vrows):
    per_seg = n_vrows // SUBLANES
    for g in range(x.shape[-1] // LANES):
        lanes = slice(g * LANES, (g + 1) * LANES)
        for i in range(x.shape[0] // SUBLANES):
            base = SUBLANES * SUBLANES * (i % per_seg) + i // per_seg
            slab[g, pl.ds(base, SUBLANES, stride=SUBLANES), :] = x[_vrow(i), lanes]
        dst[:, lanes] = slab[g]


def _from_segment_major(x, slab, dst, row0, n_vrows):
    per_seg = n_vrows // SUBLANES
    for g in range(x.shape[-1] // LANES):
        lanes = slice(g * LANES, (g + 1) * LANES)
        slab[g] = x[:, lanes]
        for i in range(x.shape[0] // SUBLANES):
            base = SUBLANES * SUBLANES * (i % per_seg) + i // per_seg
            dst[pl.ds(row0 + SUBLANES * i, SUBLANES), lanes] = slab[g, pl.ds(base, SUBLANES, stride=SUBLANES), :]


def _scan_pieces(a_s, u_s, hcar, n_vrows, chained):
    def piece(g):
        sub = _sublane_index(LANES)
        lanes = slice(g * LANES, (g + 1) * LANES)
        h = hcar[:, lanes]
        if chained:
            h = jnp.where(sub == 0, pltpu.roll(h, 1, 0), 0.0)
        prod = None
        for j in range(n_vrows):
            a = a_s[_vrow(j), lanes]
            h = a * h + u_s[_vrow(j), lanes]
            u_s[_vrow(j), lanes] = h
            if chained:
                prod = a if prod is None else prod * a
                a_s[_vrow(j), lanes] = prod
        if chained:
            seg_a, seg_h = prod, h
            for sft in (1, 2, 4):
                keep = sub >= sft
                a_sh = jnp.where(keep, pltpu.roll(seg_a, sft, 0), 1.0)
                h_sh = jnp.where(keep, pltpu.roll(seg_h, sft, 0), 0.0)
                seg_h = seg_a * h_sh + seg_h
                seg_a = seg_a * a_sh
            h_in = jnp.where(sub >= 1, pltpu.roll(seg_h, 1, 0), 0.0)
            for j in range(n_vrows):
                u_s[_vrow(j), lanes] = u_s[_vrow(j), lanes] + a_s[_vrow(j), lanes] * h_in
            h = seg_h
        hcar[:, lanes] = h

    return [functools.partial(piece, g) for g in range(a_s.shape[-1] // LANES)]


def _interleave(mxu_pieces, vpu_pieces):
    n_m, n_v = len(mxu_pieces), len(vpu_pieces)
    assert n_m > 0
    for i, piece in enumerate(mxu_pieces):
        piece()
        for vp in vpu_pieces[i * n_v // n_m:(i + 1) * n_v // n_m]:
            vp()


def _attend(q, k, v):
    k, v = k.astype(BF16), v.astype(BF16)
    sc = lax.dot_general(q, k, (((1,), (1,)), ((), ())), preferred_element_type=F32)
    p = jnp.exp2((sc - jnp.max(sc, axis=-1, keepdims=True)) * (HEAD_DIM ** -0.5 * LOG2_E))
    return _dot(p.astype(BF16), v) / jnp.sum(p, axis=-1, keepdims=True)


def _mixer_kernel(x_ref, k_ref, v_ref, ca0_ref, h0_ref, cb0_ref, lng_ref, lnb_ref,
                  w_in_ref, bg_ref, caw_ref, cab_ref, wri_ref, br_ref, bi_ref, lam_ref, pa_ref,
                  cbw_ref, cbb_ref, lnbg_ref, lnbb_ref, pb_ref, pc_ref, wo_ref, ln1g_ref, ln1b_ref,
                  x1_ref, ca_out, h_out, cb_out,
                  exta, extb, hista, histb, hcar, a_s, u_s, xc_s, vb_s, att_s, xb_s, q_s, z_s, xr_s, ypre_s, slab_s,
                  *, chained, n_vrows, n_tiles, alpha, in_ln, in_perm):
    step = pl.program_id(0)
    n_real = pl.num_programs(0) - 1
    t = step % n_tiles
    S = n_vrows

    @pl.when(step == 0)
    def _():
        ypre_s[...] = jnp.zeros(ypre_s.shape, F32)

    def prev_ln_pieces():
        chunk = (SUBLANES * S) // LN_CHUNKS

        def piece(c):
            rows = slice(c * chunk, (c + 1) * chunk)
            x1_ref[rows, :] = _layer_norm(ypre_s[rows, :], ln1g_ref[...], ln1b_ref[...])
        return [functools.partial(piece, c) for c in range(LN_CHUNKS)]

    @pl.when(step == n_real)
    def _():
        for piece in prev_ln_pieces():
            piece()

    @pl.when(step < n_real)
    def _():
        @pl.when(t == 0)
        def _():
            hista[...] = ca0_ref[...]
            histb[...] = cb0_ref[...]
            hcar[...] = h0_ref[...]

        if in_ln or in_perm:
            x_in = x_ref[...]
            if in_ln:
                x_in = _layer_norm(x_in, lng_ref[...], lnb_ref[...])
            if in_perm:
                _to_segment_major(x_in, slab_s, xr_s, S)
            else:
                xr_s[...] = x_in
            x_res = xr_s
        else:
            x_res = x_ref
        xb_s[...] = x_res[...].astype(BF16)

        def proj(c0, width):
            return _dot(xb_s[...], w_in_ref[:, c0:c0 + width])

        def proj_pieces(dst, dst0, c0, width):
            def piece(off):
                dst[:, dst0 + off:dst0 + off + MXU_COLS] = proj(c0 + off, MXU_COLS).astype(dst.dtype)
            return [functools.partial(piece, off) for off in range(0, width, MXU_COLS)]

        _interleave(proj_pieces(exta.at[_vrow(HIST_A, S), :], 0, C_XA, D_LRU), prev_ln_pieces())
        glu = proj(C_GLU, 2 * D_CONV)
        extb[_vrow(HIST_B, S), :] = glu[:, :D_CONV] * _twice_sigmoid_of_twice(glu[:, D_CONV:])
        _fill_halo(exta, hista, HIST_A, S, chained)
        _fill_halo(extb, histb, HIST_B, S, chained)

        _interleave(proj_pieces(q_s, 0, C_Q, D_XATTN), _conv_pieces(exta, caw_ref, cab_ref, xc_s, S, CONV_A_WIDTH))

        lam = lam_ref[...]
        log_sig_lam = jnp.minimum(lam, 0.0) - jnp.log1p(jnp.exp(-jnp.abs(lam)))
        decay = (0.5 * LRU_C * LOG2_E) * log_sig_lam

        def lru_gate_piece(n):
            c0, c1 = n * LRU_BLOCK, (n + 1) * LRU_BLOCK
            xc_half = xc_s[:, c0:c1]
            ri = _dot(xc_half.astype(BF16), wri_ref[n])
            r2 = _twice_sigmoid_of_twice(ri[:, :LRU_BLOCK] + br_ref[:, c0:c1])
            i2 = _twice_sigmoid_of_twice(ri[:, LRU_BLOCK:] + bi_ref[:, c0:c1])
            a = jnp.exp2(r2 * decay[:, c0:c1])
            a_s[:, c0:c1] = a
            u_s[:, c0:c1] = jnp.sqrt(1.0 - a * a) * (i2 * xc_half)

        z_pieces = proj_pieces(z_s, 0, C_YA, D_LRU) + proj_pieces(z_s, D_LRU, C_GATE, 3 * D_MODEL)
        n_gate, n_conv = len(z_pieces) // 4, 5 * len(z_pieces) // 8
        _interleave(z_pieces[:n_gate], [functools.partial(lru_gate_piece, n) for n in range(N_LRU_BLOCKS)])
        _interleave(z_pieces[n_gate:n_gate + n_conv], _conv_pieces(extb, cbw_ref, cbb_ref, vb_s, S, CONV_B_WIDTH))
        _interleave(z_pieces[n_gate + n_conv:], _scan_pieces(a_s, u_s, hcar, S, chained))

        row_seq = lax.broadcasted_iota(jnp.int32, (SUBLANES * S, HEAD_DIM), 0) % SUBLANES
        for hd in range(N_HEADS):
            c0, c1 = hd * HEAD_DIM, (hd + 1) * HEAD_DIM
            if chained:
                att = _attend(q_s[:, c0:c1], k_ref[0, :, c0:c1], v_ref[0, :, c0:c1])
            else:
                att = jnp.zeros((SUBLANES * S, HEAD_DIM), F32)
                for s in range(SUBLANES):
                    att = jnp.where(row_seq == s, _attend(q_s[:, c0:c1], k_ref[s, :, c0:c1], v_ref[s, :, c0:c1]), att)
            att_s[:, c0:c1] = att

        chunk = (SUBLANES * S) // TAIL_CHUNKS
        for c in range(TAIL_CHUNKS):
            rows = slice(c * chunk, (c + 1) * chunk)

            def gate2(i):
                cols = slice(D_LRU + i * D_MODEL, D_LRU + (i + 1) * D_MODEL)
                return _twice_sigmoid_of_twice(z_s[rows, cols] + bg_ref[i:i + 1, :])

            out_a = _dot((u_s[rows, :] * _twice_gelu_tanh(z_s[rows, :D_LRU])).astype(BF16), pa_ref[...])
            merged2 = gate2(0) * out_a
            vn_half = _layer_norm(vb_s[rows, :], lnbg_ref[...], lnbb_ref[...])
            out_b = _dot((vn_half * _twice_sigmoid_of_twice(vn_half)).astype(BF16), pb_ref[...])
            merged2 = merged2 + gate2(1) * out_b
            out_c = _dot(att_s[rows, :].astype(BF16), pc_ref[...])
            merged2 = merged2 + gate2(2) * out_c
            y = _dot(merged2.astype(BF16), wo_ref[...])
            ypre_s[rows, :] = alpha * x_res[rows, :] + y

        @pl.when(t == n_tiles - 1)
        def _():
            ca_out[...] = hista[...]
            cb_out[...] = histb[...]
            h_out[...] = hcar[...]


def _ffn_kernel(x_ref, wg_ref, wu_ref, wd_ref, g_ref, b_ref, o_ref, slab_s, *, alpha, sub_rows, out_perm):
    for r0 in range(0, x_ref.shape[0], sub_rows):
        rows = slice(r0, r0 + sub_rows)
        x = x_ref[rows, :]
        xb = x.astype(BF16)
        gate_half = _dot(xb, wg_ref[...])
        up = _dot(xb, wu_ref[...])
        hid = (gate_half * _twice_sigmoid_of_twice(gate_half) * up).astype(BF16)
        y = _layer_norm(alpha * x + _dot(hid, wd_ref[...]), g_ref[...], b_ref[...])
        if out_perm:
            _from_segment_major(y, slab_s, o_ref, r0, sub_rows // SUBLANES)
        else:
            o_ref[rows, :] = y


def _kv_kernel(m_ref, wk_ref, wv_ref, k_ref, v_ref, kb_ref, vb_ref):
    mb = m_ref[...].astype(BF16)
    k = _dot(mb, wk_ref[...])
    v = _dot(mb, wv_ref[...])
    k_ref[...] = k
    v_ref[...] = v
    kb_ref[...] = k.astype(BF16)
    vb_ref[...] = v.astype(BF16)


def _const_spec(shape, layer=None):
    if layer is None:
        return pl.BlockSpec(shape, lambda *_: (0,) * len(shape), pipeline_mode=pl.Buffered(1))
    return pl.BlockSpec((None,) + shape, lambda *_: (layer,) + (0,) * len(shape), pipeline_mode=pl.Buffered(1))


def _row_tile(n_rows, want):
    tile = min(want, n_rows)
    assert n_rows % tile == 0 and tile % SUBLANES == 0
    return tile


def _kv_call(mem, wk, wv):
    depth = wk.shape[0]
    bsz = mem.shape[0]
    out_f32 = jax.ShapeDtypeStruct((depth, bsz, N_MEM, D_XATTN), F32)
    out_bf16 = jax.ShapeDtypeStruct((depth, bsz, N_MEM, D_XATTN), BF16)
    w_spec = pl.BlockSpec((None, D_MODEL, D_XATTN), lambda l, b: (l, 0, 0))
    o_spec = pl.BlockSpec((None, None, N_MEM, D_XATTN), lambda l, b: (l, b, 0, 0))
    return pl.pallas_call(
        _kv_kernel,
        grid=(depth, bsz),
        in_specs=[pl.BlockSpec((None, N_MEM, D_MODEL), lambda l, b: (b, 0, 0)), w_spec, w_spec],
        out_specs=[o_spec, o_spec, o_spec, o_spec],
        out_shape=[out_f32, out_f32, out_bf16, out_bf16],
        compiler_params=pltpu.CompilerParams(dimension_semantics=("arbitrary", "arbitrary")),
        name="mem_kv",
    )(mem, wk, wv)


def _mixer_call(x2d, kb, vb, ca0, h0, cb0, ln_in, lw, layer, *, chained, n_blocks, n_tiles, n_vrows, alpha,
                in_ln, in_perm, name):
    S = n_vrows
    tile = SUBLANES * S
    n_mem_sets = 1 if chained else SUBLANES
    assert chained or n_tiles == 1
    assert not chained or S >= HIST_B
    assert not in_perm or S % SUBLANES == 0
    kernel = functools.partial(_mixer_kernel, chained=chained, n_vrows=S, n_tiles=n_tiles, alpha=alpha,
                               in_ln=in_ln, in_perm=in_perm)
    n_real = n_blocks * n_tiles
    x_spec = pl.BlockSpec((tile, D_MODEL), lambda i: (jnp.minimum(i, n_real - 1), 0))
    o_spec = pl.BlockSpec((tile, D_MODEL), lambda i: (jnp.maximum(i - 1, 0), 0))

    def block_of(i):
        return jnp.minimum(i, n_real - 1) // n_tiles

    def per_block(shape):
        return pl.BlockSpec((None,) + shape, lambda i: (block_of(i),) + (0,) * len(shape))

    state_shapes = [(SUBLANES * HIST_A, D_LRU), (SUBLANES, D_LRU), (SUBLANES * HIST_B, D_CONV)]
    in_specs = [
        x_spec,
        pl.BlockSpec((n_mem_sets, N_MEM, D_XATTN), lambda i: (block_of(i), 0, 0)),
        pl.BlockSpec((n_mem_sets, N_MEM, D_XATTN), lambda i: (block_of(i), 0, 0)),
        *[per_block(s) for s in state_shapes],
        _const_spec((1, D_MODEL)),
        _const_spec((1, D_MODEL)),
        _const_spec((D_MODEL, lw["w_in"].shape[-1]), layer),
        _const_spec((3, D_MODEL), layer),
        _const_spec((SUBLANES * CONV_A_WIDTH, D_LRU), layer),
        _const_spec((SUBLANES, D_LRU), layer),
        _const_spec((N_LRU_BLOCKS, LRU_BLOCK, 2 * LRU_BLOCK), layer),
        _const_spec((1, D_LRU), layer),
        _const_spec((1, D_LRU), layer),
        _const_spec((1, D_LRU), layer),
        _const_spec((D_LRU, D_MODEL), layer),
        _const_spec((SUBLANES * CONV_B_WIDTH, D_CONV), layer),
        _const_spec((SUBLANES, D_CONV), layer),
        _const_spec((1, D_CONV), layer),
        _const_spec((1, D_CONV), layer),
        _const_spec((D_CONV, D_MODEL), layer),
        _const_spec((D_XATTN, D_MODEL), layer),
        _const_spec((D_MODEL, D_MODEL), layer),
        _const_spec((1, D_MODEL), layer),
        _const_spec((1, D_MODEL), layer),
    ]
    out_specs = [o_spec] + [per_block(s) for s in state_shapes]
    out_shape = [jax.ShapeDtypeStruct(x2d.shape, F32)] + [
        jax.ShapeDtypeStruct((n_blocks,) + s, F32) for s in state_shapes]
    scratch = [
        pltpu.VMEM((SUBLANES * (HIST_A + S), D_LRU), F32),
        pltpu.VMEM((SUBLANES * (HIST_B + S), D_CONV), F32),
        pltpu.VMEM(state_shapes[0], F32),
        pltpu.VMEM(state_shapes[2], F32),
        pltpu.VMEM(state_shapes[1], F32),
        pltpu.VMEM((tile, D_LRU), F32),
        pltpu.VMEM((tile, D_LRU), F32),
        pltpu.VMEM((tile, D_LRU), F32),
        pltpu.VMEM((tile, D_CONV), F32),
        pltpu.VMEM((tile, D_XATTN), F32),
        pltpu.VMEM((tile, D_MODEL), BF16),
        pltpu.VMEM((tile, D_XATTN), BF16),
        pltpu.VMEM((tile, D_LRU + 3 * D_MODEL), F32),
        pltpu.VMEM((tile, D_MODEL) if (in_ln or in_perm) else (SUBLANES, LANES), F32),
        pltpu.VMEM((tile, D_MODEL), F32),
        pltpu.VMEM((D_MODEL // LANES, tile, LANES) if in_perm else (1, SUBLANES, LANES), F32),
    ]
    return pl.pallas_call(
        kernel,
        grid=(n_real + 1,),
        in_specs=in_specs,
        out_specs=out_specs,
        out_shape=out_shape,
        scratch_shapes=scratch,
        compiler_params=pltpu.CompilerParams(dimension_semantics=("arbitrary",),
                                             vmem_limit_bytes=VMEM_LIMIT),
        name=name,
    )(x2d, kb, vb, ca0, h0, cb0, ln_in[0], ln_in[1],
      lw["w_in"], lw["b_gate"], lw["conv_a_w"], lw["conv_a_b"], lw["w_ri"], lw["lru_b_r"], lw["lru_b_i"],
      lw["lru_lambda"], lw["proj_a"], lw["conv_b_w"], lw["conv_b_b"], lw["ln_b_g"], lw["ln_b_b"],
      lw["proj_b"], lw["proj_c"], lw["w_out"], lw["ln1_g"], lw["ln1_b"])


def _ffn_call(x2d, lw, layer, *, tile, alpha, out_perm, name):
    rows = x2d.shape[0]
    d_ff = lw["w_ffn_gate"].shape[-1]
    sub_rows = min(FFN_SUB_ROWS, tile)
    assert tile % sub_rows == 0
    x_spec = pl.BlockSpec((tile, D_MODEL), lambda i: (i, 0))
    return pl.pallas_call(
        functools.partial(_ffn_kernel, alpha=alpha, sub_rows=sub_rows, out_perm=out_perm),
        grid=(rows // tile,),
        in_specs=[x_spec,
                  _const_spec((D_MODEL, d_ff), layer), _const_spec((D_MODEL, d_ff), layer),
                  _const_spec((d_ff, D_MODEL), layer),
                  _const_spec((1, D_MODEL), layer), _const_spec((1, D_MODEL), layer)],
        out_specs=x_spec,
        out_shape=jax.ShapeDtypeStruct(x2d.shape, F32),
        scratch_shapes=[pltpu.VMEM((D_MODEL // LANES, sub_rows, LANES) if out_perm else (1, SUBLANES, LANES), F32)],
        compiler_params=pltpu.CompilerParams(dimension_semantics=("arbitrary",),
                                             vmem_limit_bytes=VMEM_LIMIT),
        name=name,
    )(x2d, lw["w_ffn_gate"], lw["w_ffn_up"], lw["w_ffn_down"], lw["ln2_g"], lw["ln2_b"])


def _steps_to_segments(state):
    n_seq, steps, n_ch = state.shape
    return state.transpose(1, 0, 2).reshape(1, steps * n_seq, n_ch)


def _segments_to_steps(state, steps):
    n_ch = state.shape[-1]
    return state.reshape(steps, SUBLANES, n_ch).transpose(1, 0, 2)


def kernel(x_prompt, x_sample, mem_prompt, state_conv_a, state_lru, state_conv_b, cache_mem_k, cache_mem_v,
           ln_in_g, ln_in_b, w_in, b_gate, conv_a_w, conv_a_b, lru_w_r, lru_b_r, lru_w_i, lru_b_i, lru_lambda,
           proj_a, conv_b_w, conv_b_b, ln_b_g, ln_b_b, proj_b, w_mem_k, w_mem_v, proj_c, w_out, ln1_g, ln1_b,
           w_ffn_gate, w_ffn_up, w_ffn_down, ln2_g, ln2_b):
    depth = w_in.shape[0]
    bp, tp, _ = x_prompt.shape
    bs, ts, _ = x_sample.shape
    assert bs == SUBLANES, "the sample group maps its sequences onto the 8 sublanes"
    alpha = (2 * depth) ** 0.25

    row = lambda p: p.reshape(depth, 1, p.shape[-1])
    rep = lambda p, n: jnp.repeat(p.reshape(depth, -1, p.shape[-1]), n, axis=1)
    half_cols = jnp.concatenate([jnp.full((C_GLU,), 1.0, F32), jnp.full((2 * D_CONV,), 0.5, F32),
                                 jnp.full((D_XATTN,), 1.0, F32), jnp.full((3 * D_MODEL,), 0.5, F32)])
    lw = {
        "w_in": (w_in * half_cols).astype(BF16), "b_gate": 0.5 * b_gate,
        "conv_a_w": rep(0.5 * conv_a_w, SUBLANES), "conv_a_b": rep(0.5 * conv_a_b, SUBLANES),
        "w_ri": jnp.concatenate([lru_w_r, lru_w_i], axis=-1).astype(BF16),
        "lru_b_r": 0.5 * lru_b_r.reshape(depth, 1, D_LRU), "lru_b_i": 0.5 * lru_b_i.reshape(depth, 1, D_LRU),
        "lru_lambda": row(lru_lambda), "proj_a": (0.5 * proj_a).astype(BF16),
        "conv_b_w": rep(conv_b_w, SUBLANES), "conv_b_b": rep(conv_b_b, SUBLANES),
        "ln_b_g": 0.5 * row(ln_b_g), "ln_b_b": 0.5 * row(ln_b_b),
        "proj_b": proj_b.astype(BF16), "proj_c": proj_c.astype(BF16), "w_out": (0.5 * w_out).astype(BF16),
        "ln1_g": row(ln1_g), "ln1_b": row(ln1_b),
        "w_ffn_gate": (0.5 * w_ffn_gate).astype(BF16), "w_ffn_up": w_ffn_up.astype(BF16),
        "w_ffn_down": w_ffn_down.astype(BF16), "ln2_g": row(ln2_g), "ln2_b": row(ln2_b),
    }
    ln_in = (ln_in_g.reshape(1, D_MODEL), ln_in_b.reshape(1, D_MODEL))

    mem_k, mem_v, mem_kb, mem_vb = _kv_call(mem_prompt, w_mem_k.astype(BF16), w_mem_v.astype(BF16))
    cache_k = cache_mem_k.reshape(depth, bs, N_MEM, D_XATTN)
    cache_v = cache_mem_v.reshape(depth, bs, N_MEM, D_XATTN)

    sp = 32
    assert tp % (SUBLANES * sp) == 0 and FFN_SUB_ROWS == SUBLANES * sp
    n_tiles = tp // (SUBLANES * sp)
    xp = x_prompt.reshape(bp * tp, D_MODEL)
    xs = x_sample.transpose(1, 0, 2).reshape(ts * bs, D_MODEL)

    zero_ca = jnp.zeros((bp, SUBLANES * HIST_A, D_LRU), F32)
    zero_h = jnp.zeros((bp, SUBLANES, D_LRU), F32)
    zero_cb = jnp.zeros((bp, SUBLANES * HIST_B, D_CONV), F32)

    ca_p, h_p, cb_p, ca_s, h_s, cb_s = [], [], [], [], [], []
    for l in range(depth):
        first, last = l == 0, l == depth - 1
        xp, ca, hl, cb = _mixer_call(xp, mem_kb[l], mem_vb[l], zero_ca, zero_h, zero_cb, ln_in, lw, l,
                                     chained=True, n_blocks=bp, n_tiles=n_tiles, n_vrows=sp, alpha=alpha,
                                     in_ln=first, in_perm=first, name="mixer_prompt")
        xp = _ffn_call(xp, lw, l, tile=_row_tile(bp * tp, 1024), alpha=alpha, out_perm=last, name="ffn_prompt")
        ca_p.append(ca.reshape(bp, HIST_A, SUBLANES, D_LRU)[:, :, SUBLANES - 1])
        h_p.append(hl[:, SUBLANES - 1])
        cb_p.append(cb.reshape(bp, HIST_B, SUBLANES, D_CONV)[:, :, SUBLANES - 1])
        xs, ca, hl, cb = _mixer_call(xs, cache_k[l], cache_v[l], _steps_to_segments(state_conv_a[l]),
                                     state_lru[l][None], _steps_to_segments(state_conv_b[l]), ln_in, lw, l,
                                     chained=False, n_blocks=1, n_tiles=1, n_vrows=ts, alpha=alpha,
                                     in_ln=first, in_perm=False, name="mixer_sample")
        xs = _ffn_call(xs, lw, l, tile=bs * ts, alpha=alpha, out_perm=False, name="ffn_sample")
        ca_s.append(_segments_to_steps(ca[0], HIST_A))
        h_s.append(hl[0])
        cb_s.append(_segments_to_steps(cb[0], HIST_B))

    y_prompt = xp.reshape(bp, tp, D_MODEL)
    y_sample = xs.reshape(ts, bs, D_MODEL).transpose(1, 0, 2)
    kv_shape = (depth, bp, N_MEM, N_HEADS, HEAD_DIM)
    return (y_prompt, y_sample,
            jnp.stack(ca_p), jnp.stack(h_p), jnp.stack(cb_p),
            mem_k.reshape(kv_shape), mem_v.reshape(kv_shape),
            jnp.stack(ca_s), jnp.stack(h_s), jnp.stack(cb_s))
```

```python
import functools

import jax
import jax.numpy as jnp
from jax import lax
from jax.experimental import pallas as pl
from jax.experimental.pallas import tpu as pltpu

D_MODEL = 1024
N_MEM = 256
D_LRU = 1024
N_LRU_BLOCKS = 8
LRU_BLOCK = D_LRU // N_LRU_BLOCKS
CONV_A_WIDTH = 4
LRU_C = 8.0
D_CONV = 512
CONV_B_WIDTH = 31
N_HEADS = 4
HEAD_DIM = 128
D_XATTN = N_HEADS * HEAD_DIM
LN_EPS = 1e-5
LOG2_E = 1.4426950408889634

C_XA = 0
C_YA = D_LRU
C_GLU = 2 * D_LRU
C_Q = 2 * D_LRU + 2 * D_CONV
C_GATE = C_Q + D_XATTN

SUBLANES = 8
LANES = 128
HIST_A = CONV_A_WIDTH - 1
HIST_B = CONV_B_WIDTH - 1
CONV_CHUNK = 8
TAP_BLOCK = 8
MXU_COLS = 256
FFN_SUB_ROWS = 256
TAIL_CHUNKS = 1
LN_CHUNKS = 4
VMEM_LIMIT = 52 * 1024 * 1024

BF16 = jnp.bfloat16
F32 = jnp.float32


def _dot(a, b):
    return jnp.dot(a, b, preferred_element_type=F32)


def _twice_sigmoid_of_twice(x):
    return jnp.tanh(x) + 1.0


def _layer_norm(x, g, b):
    mu = jnp.mean(x, axis=-1, keepdims=True)
    xc = x - mu
    var = jnp.mean(xc * xc, axis=-1, keepdims=True)
    return xc * lax.rsqrt(var + LN_EPS) * g + b


def _twice_gelu_tanh(x):
    return x * (1.0 + jnp.tanh(x * (0.7978845608028654 + 0.035677408136300125 * (x * x))))


def _vrow(j, n=1):
    return slice(SUBLANES * j, SUBLANES * (j + n))


def _sublane_index(n_ch):
    return lax.broadcasted_iota(jnp.int32, (SUBLANES, n_ch), 0)


def _fill_halo(ext, hist, n_hist, n_vrows, chained):
    sub = _sublane_index(ext.shape[-1])
    for i in range(n_hist):
        prev = hist[_vrow(i), :]
        if chained:
            cur = ext[_vrow(n_vrows + i), :]
            ext[_vrow(i), :] = pltpu.roll(jnp.where(sub == SUBLANES - 1, prev, cur), 1, 0)
        else:
            ext[_vrow(i), :] = prev
    for i in range(n_hist):
        hist[_vrow(i), :] = ext[_vrow(n_vrows + i), :]


def _conv_pieces(ext, w_ref, b_ref, out_ref, n_vrows, width):
    def piece(g, j0, ch):
        lanes = slice(g * LANES, (g + 1) * LANES)
        accs = [b_ref[:, lanes]] * ch
        for k0 in range(0, width, TAP_BLOCK):
            kn = min(TAP_BLOCK, width - k0)
            rows = [ext[_vrow(j0 + k0 + m), lanes] for m in range(ch + kn - 1)]
            for k in range(kn):
                wk = w_ref[_vrow(k0 + k), lanes]
                accs = [accs[i] + wk * rows[i + k] for i in range(ch)]
        for i in range(ch):
            out_ref[_vrow(j0 + i), lanes] = accs[i]

    return [functools.partial(piece, g, j0, min(CONV_CHUNK, n_vrows - j0))
            for g in range(ext.shape[-1] // LANES) for j0 in range(0, n_vrows, CONV_CHUNK)]


def _to_segment_major(x, slab, dst, n_vrows):
    per_seg = n_vrows // SUBLANES
    for g in range(x.shape[-1] // LANES):
        lanes = slice(g * LANES, (g + 1) * LANES)
        for i in range(x.shape[0] // SUBLANES):
            base = SUBLANES * SUBLANES * (i % per_seg) + i // per_seg
            slab[g, pl.ds(base, SUBLANES, stride=SUBLANES), :] = x[_vrow(i), lanes]
        dst[:, lanes] = slab[g]


def _from_segment_major(x, slab, dst, row0, n_vrows):
    per_seg = n_vrows // SUBLANES
    for g in range(x.shape[-1] // LANES):
        lanes = slice(g * LANES, (g + 1) * LANES)
        slab[g] = x[:, lanes]
        for i in range(x.shape[0] // SUBLANES):
            base = SUBLANES * SUBLANES * (i % per_seg) + i // per_seg
            dst[pl.ds(row0 + SUBLANES * i, SUBLANES), lanes] = slab[g, pl.ds(base, SUBLANES, stride=SUBLANES), :]


def _scan_pieces(a_s, u_s, hcar, n_vrows, chained):
    def piece(g):
        sub = _sublane_index(LANES)
        lanes = slice(g * LANES, (g + 1) * LANES)
        h = hcar[:, lanes]
        if chained:
            h = jnp.where(sub == 0, pltpu.roll(h, 1, 0), 0.0)
        prod = None
        for j in range(n_vrows):
            a = a_s[_vrow(j), lanes]
            h = a * h + u_s[_vrow(j), lanes]
            u_s[_vrow(j), lanes] = h
            if chained:
                prod = a if prod is None else prod * a
                a_s[_vrow(j), lanes] = prod
        if chained:
            seg_a, seg_h = prod, h
            for sft in (1, 2, 4):
                keep = sub >= sft
                a_sh = jnp.where(keep, pltpu.roll(seg_a, sft, 0), 1.0)
                h_sh = jnp.where(keep, pltpu.roll(seg_h, sft, 0), 0.0)
                seg_h = seg_a * h_sh + seg_h
                seg_a = seg_a * a_sh
            h_in = jnp.where(sub >= 1, pltpu.roll(seg_h, 1, 0), 0.0)
            for j in range(n_vrows):
                u_s[_vrow(j), lanes] = u_s[_vrow(j), lanes] + a_s[_vrow(j), lanes] * h_in
            h = seg_h
        hcar[:, lanes] = h

    return [functools.partial(piece, g) for g in range(a_s.shape[-1] // LANES)]


def _interleave(mxu_pieces, vpu_pieces):
    n_m, n_v = len(mxu_pieces), len(vpu_pieces)
    assert n_m > 0
    for i, piece in enumerate(mxu_pieces):
        piece()
        for vp in vpu_pieces[i * n_v // n_m:(i + 1) * n_v // n_m]:
            vp()


def _attend(q, k, v):
    k, v = k.astype(BF16), v.astype(BF16)
    sc = lax.dot_general(q, k, (((1,), (1,)), ((), ())), preferred_element_type=F32)
    p = jnp.exp2((sc - jnp.max(sc, axis=-1, keepdims=True)) * (HEAD_DIM ** -0.5 * LOG2_E))
    return _dot(p.astype(BF16), v) * (1.0 / jnp.sum(p, axis=-1, keepdims=True))


def _mixer_kernel(x_ref, k_ref, v_ref, ca0_ref, h0_ref, cb0_ref, lng_ref, lnb_ref,
                  w_in_ref, bg_ref, caw_ref, cab_ref, wri_ref, br_ref, bi_ref, lam_ref, pa_ref,
                  cbw_ref, cbb_ref, lnbg_ref, lnbb_ref, pb_ref, pc_ref, wo_ref, ln1g_ref, ln1b_ref,
                  x1_ref, ca_out, h_out, cb_out,
                  exta, extb, hista, histb, hcar, a_s, u_s, xc_s, vb_s, att_s, xb_s, q_s, z_s, xr_s, ypre_s, slab_s,
                  *, chained, n_vrows, n_tiles, alpha, in_ln, in_perm):
    step = pl.program_id(0)
    n_real = pl.num_programs(0) - 1
    t = step % n_tiles
    S = n_vrows

    @pl.when(step == 0)
    def _():
        ypre_s[...] = jnp.zeros(ypre_s.shape, F32)

    def prev_ln_pieces():
        chunk = (SUBLANES * S) // LN_CHUNKS

        def piece(c):
            rows = slice(c * chunk, (c + 1) * chunk)
            x1_ref[rows, :] = _layer_norm(ypre_s[rows, :], ln1g_ref[...], ln1b_ref[...])
        return [functools.partial(piece, c) for c in range(LN_CHUNKS)]

    @pl.when(step == n_real)
    def _():
        for piece in prev_ln_pieces():
            piece()

    @pl.when(step < n_real)
    def _():
        @pl.when(t == 0)
        def _():
            hista[...] = ca0_ref[...]
            histb[...] = cb0_ref[...]
            hcar[...] = h0_ref[...]

        if in_ln or in_perm:
            x_in = x_ref[...]
            if in_ln:
                x_in = _layer_norm(x_in, lng_ref[...], lnb_ref[...])
            if in_perm:
                _to_segment_major(x_in, slab_s, xr_s, S)
            else:
                xr_s[...] = x_in
            x_res = xr_s
        else:
            x_res = x_ref
        xb_s[...] = x_res[...].astype(BF16)

        def proj(c0, width):
            return _dot(xb_s[...], w_in_ref[:, c0:c0 + width])

        def proj_pieces(dst, dst0, c0, width):
            def piece(off):
                dst[:, dst0 + off:dst0 + off + MXU_COLS] = proj(c0 + off, MXU_COLS).astype(dst.dtype)
            return [functools.partial(piece, off) for off in range(0, width, MXU_COLS)]

        _interleave(proj_pieces(exta.at[_vrow(HIST_A, S), :], 0, C_XA, D_LRU), prev_ln_pieces())
        glu = proj(C_GLU, 2 * D_CONV)
        extb[_vrow(HIST_B, S), :] = glu[:, :D_CONV] * _twice_sigmoid_of_twice(glu[:, D_CONV:])
        _fill_halo(exta, hista, HIST_A, S, chained)
        _fill_halo(extb, histb, HIST_B, S, chained)

        _interleave(proj_pieces(q_s, 0, C_Q, D_XATTN), _conv_pieces(exta, caw_ref, cab_ref, xc_s, S, CONV_A_WIDTH))

        lam = lam_ref[...]
        log_sig_lam = jnp.minimum(lam, 0.0) - jnp.log1p(jnp.exp(-jnp.abs(lam)))
        decay = (0.5 * LRU_C * LOG2_E) * log_sig_lam

        def lru_gate_piece(n):
            c0, c1 = n * LRU_BLOCK, (n + 1) * LRU_BLOCK
            xc_half = xc_s[:, c0:c1]
            ri = _dot(xc_half.astype(BF16), wri_ref[n])
            r2 = _twice_sigmoid_of_twice(ri[:, :LRU_BLOCK] + br_ref[:, c0:c1])
            i2 = _twice_sigmoid_of_twice(ri[:, LRU_BLOCK:] + bi_ref[:, c0:c1])
            a = jnp.exp2(r2 * decay[:, c0:c1])
            a_s[:, c0:c1] = a
            u_s[:, c0:c1] = jnp.sqrt(1.0 - a * a) * (i2 * xc_half)

        z_pieces = proj_pieces(z_s, 0, C_YA, D_LRU) + proj_pieces(z_s, D_LRU, C_GATE, 3 * D_MODEL)
        n_gate, n_conv = len(z_pieces) // 4, 5 * len(z_pieces) // 8
        _interleave(z_pieces[:n_gate], [functools.partial(lru_gate_piece, n) for n in range(N_LRU_BLOCKS)])
        _interleave(z_pieces[n_gate:n_gate + n_conv], _conv_pieces(extb, cbw_ref, cbb_ref, vb_s, S, CONV_B_WIDTH))
        _interleave(z_pieces[n_gate + n_conv:], _scan_pieces(a_s, u_s, hcar, S, chained))

        row_seq = lax.broadcasted_iota(jnp.int32, (SUBLANES * S, HEAD_DIM), 0) % SUBLANES
        for hd in range(N_HEADS):
            c0, c1 = hd * HEAD_DIM, (hd + 1) * HEAD_DIM
            if chained:
                att = _attend(q_s[:, c0:c1], k_ref[0, :, c0:c1], v_ref[0, :, c0:c1])
            else:
                att = jnp.zeros((SUBLANES * S, HEAD_DIM), F32)
                for s in range(SUBLANES):
                    att = jnp.where(row_seq == s, _attend(q_s[:, c0:c1], k_ref[s, :, c0:c1], v_ref[s, :, c0:c1]), att)
            att_s[:, c0:c1] = att

        chunk = (SUBLANES * S) // TAIL_CHUNKS
        for c in range(TAIL_CHUNKS):
            rows = slice(c * chunk, (c + 1) * chunk)

            def gate2(i):
                cols = slice(D_LRU + i * D_MODEL, D_LRU + (i + 1) * D_MODEL)
                return _twice_sigmoid_of_twice(z_s[rows, cols] + bg_ref[i:i + 1, :])

            out_a = _dot((u_s[rows, :] * _twice_gelu_tanh(z_s[rows, :D_LRU])).astype(BF16), pa_ref[...])
            merged2 = gate2(0) * out_a
            vn_half = _layer_norm(vb_s[rows, :], lnbg_ref[...], lnbb_ref[...])
            out_b = _dot((vn_half * _twice_sigmoid_of_twice(vn_half)).astype(BF16), pb_ref[...])
            merged2 = merged2 + gate2(1) * out_b
            out_c = _dot(att_s[rows, :].astype(BF16), pc_ref[...])
            merged2 = merged2 + gate2(2) * out_c
            y = _dot(merged2.astype(BF16), wo_ref[...])
            ypre_s[rows, :] = alpha * x_res[rows, :] + y

        @pl.when(t == n_tiles - 1)
        def _():
            ca_out[...] = hista[...]
            cb_out[...] = histb[...]
            h_out[...] = hcar[...]


def _ffn_kernel(x_ref, wg_ref, wu_ref, wd_ref, g_ref, b_ref, o_ref, slab_s, *, alpha, sub_rows, out_perm):
    for r0 in range(0, x_ref.shape[0], sub_rows):
        rows = slice(r0, r0 + sub_rows)
        x = x_ref[rows, :]
        xb = x.astype(BF16)
        gate_half = _dot(xb, wg_ref[...])
        up = _dot(xb, wu_ref[...])
        hid = (gate_half * _twice_sigmoid_of_twice(gate_half) * up).astype(BF16)
        y = _layer_norm(alpha * x + _dot(hid, wd_ref[...]), g_ref[...], b_ref[...])
        if out_perm:
            _from_segment_major(y, slab_s, o_ref, r0, sub_rows // SUBLANES)
        else:
            o_ref[rows, :] = y


def _kv_kernel(m_ref, wk_ref, wv_ref, k_ref, v_ref, kb_ref, vb_ref):
    mb = m_ref[...].astype(BF16)
    for l in range(wk_ref.shape[0]):
        k = _dot(mb, wk_ref[l])
        v = _dot(mb, wv_ref[l])
        k_ref[l] = k
        v_ref[l] = v
        kb_ref[l] = k.astype(BF16)
        vb_ref[l] = v.astype(BF16)


def _const_spec(shape, layer=None):
    if layer is None:
        return pl.BlockSpec(shape, lambda *_: (0,) * len(shape), pipeline_mode=pl.Buffered(1))
    return pl.BlockSpec((None,) + shape, lambda *_: (layer,) + (0,) * len(shape), pipeline_mode=pl.Buffered(1))


def _row_tile(n_rows, want):
    tile = min(want, n_rows)
    assert n_rows % tile == 0 and tile % SUBLANES == 0
    return tile


def _kv_call(mem, wk, wv):
    depth = wk.shape[0]
    bsz = mem.shape[0]
    out_f32 = jax.ShapeDtypeStruct((depth, bsz, N_MEM, D_XATTN), F32)
    out_bf16 = jax.ShapeDtypeStruct((depth, bsz, N_MEM, D_XATTN), BF16)
    o_spec = pl.BlockSpec((depth, None, N_MEM, D_XATTN), lambda b: (0, b, 0, 0))
    return pl.pallas_call(
        _kv_kernel,
        grid=(bsz,),
        in_specs=[pl.BlockSpec((None, N_MEM, D_MODEL), lambda b: (b, 0, 0)),
                  _const_spec(wk.shape), _const_spec(wv.shape)],
        out_specs=[o_spec, o_spec, o_spec, o_spec],
        out_shape=[out_f32, out_f32, out_bf16, out_bf16],
        compiler_params=pltpu.CompilerParams(dimension_semantics=("arbitrary",)),
        name="mem_kv",
    )(mem, wk, wv)


def _mixer_call(x2d, kb, vb, ca0, h0, cb0, ln_in, lw, layer, *, chained, n_blocks, n_tiles, n_vrows, alpha,
                in_ln, in_perm, name):
    S = n_vrows
    tile = SUBLANES * S
    n_mem_sets = 1 if chained else SUBLANES
    assert chained or n_tiles == 1
    assert not chained or S >= HIST_B
    assert not in_perm or S % SUBLANES == 0
    kernel = functools.partial(_mixer_kernel, chained=chained, n_vrows=S, n_tiles=n_tiles, alpha=alpha,
                               in_ln=in_ln, in_perm=in_perm)
    n_real = n_blocks * n_tiles
    x_spec = pl.BlockSpec((tile, D_MODEL), lambda i: (jnp.minimum(i, n_real - 1), 0))
    o_spec = pl.BlockSpec((tile, D_MODEL), lambda i: (jnp.maximum(i - 1, 0), 0))

    def block_of(i):
        return jnp.minimum(i, n_real - 1) // n_tiles

    def per_block(shape):
        return pl.BlockSpec((None,) + shape, lambda i: (block_of(i),) + (0,) * len(shape))

    state_shapes = [(SUBLANES * HIST_A, D_LRU), (SUBLANES, D_LRU), (SUBLANES * HIST_B, D_CONV)]
    in_specs = [
        x_spec,
        pl.BlockSpec((n_mem_sets, N_MEM, D_XATTN), lambda i: (block_of(i), 0, 0)),
        pl.BlockSpec((n_mem_sets, N_MEM, D_XATTN), lambda i: (block_of(i), 0, 0)),
        *[per_block(s) for s in state_shapes],
        _const_spec((1, D_MODEL)),
        _const_spec((1, D_MODEL)),
        _const_spec((D_MODEL, lw["w_in"].shape[-1]), layer),
        _const_spec((3, D_MODEL), layer),
        _const_spec((SUBLANES * CONV_A_WIDTH, D_LRU), layer),
        _const_spec((SUBLANES, D_LRU), layer),
        _const_spec((N_LRU_BLOCKS, LRU_BLOCK, 2 * LRU_BLOCK), layer),
        _const_spec((1, D_LRU), layer),
        _const_spec((1, D_LRU), layer),
        _const_spec((1, D_LRU), layer),
        _const_spec((D_LRU, D_MODEL), layer),
        _const_spec((SUBLANES * CONV_B_WIDTH, D_CONV), layer),
        _const_spec((SUBLANES, D_CONV), layer),
        _const_spec((1, D_CONV), layer),
        _const_spec((1, D_CONV), layer),
        _const_spec((D_CONV, D_MODEL), layer),
        _const_spec((D_XATTN, D_MODEL), layer),
        _const_spec((D_MODEL, D_MODEL), layer),
        _const_spec((1, D_MODEL), layer),
        _const_spec((1, D_MODEL), layer),
    ]
    out_specs = [o_spec] + [per_block(s) for s in state_shapes]
    out_shape = [jax.ShapeDtypeStruct(x2d.shape, F32)] + [
        jax.ShapeDtypeStruct((n_blocks,) + s, F32) for s in state_shapes]
    scratch = [
        pltpu.VMEM((SUBLANES * (HIST_A + S), D_LRU), F32),
        pltpu.VMEM((SUBLANES * (HIST_B + S), D_CONV), F32),
        pltpu.VMEM(state_shapes[0], F32),
        pltpu.VMEM(state_shapes[2], F32),
        pltpu.VMEM(state_shapes[1], F32),
        pltpu.VMEM((tile, D_LRU), F32),
        pltpu.VMEM((tile, D_LRU), F32),
        pltpu.VMEM((tile, D_LRU), F32),
        pltpu.VMEM((tile, D_CONV), F32),
        pltpu.VMEM((tile, D_XATTN), F32),
        pltpu.VMEM((tile, D_MODEL), BF16),
        pltpu.VMEM((tile, D_XATTN), BF16),
        pltpu.VMEM((tile, D_LRU + 3 * D_MODEL), F32),
        pltpu.VMEM((tile, D_MODEL) if (in_ln or in_perm) else (SUBLANES, LANES), F32),
        pltpu.VMEM((tile, D_MODEL), F32),
        pltpu.VMEM((D_MODEL // LANES, tile, LANES) if in_perm else (1, SUBLANES, LANES), F32),
    ]
    return pl.pallas_call(
        kernel,
        grid=(n_real + 1,),
        in_specs=in_specs,
        out_specs=out_specs,
        out_shape=out_shape,
        scratch_shapes=scratch,
        compiler_params=pltpu.CompilerParams(dimension_semantics=("arbitrary",),
                                             vmem_limit_bytes=VMEM_LIMIT),
        name=name,
    )(x2d, kb, vb, ca0, h0, cb0, ln_in[0], ln_in[1],
      lw["w_in"], lw["b_gate"], lw["conv_a_w"], lw["conv_a_b"], lw["w_ri"], lw["lru_b_r"], lw["lru_b_i"],
      lw["lru_lambda"], lw["proj_a"], lw["conv_b_w"], lw["conv_b_b"], lw["ln_b_g"], lw["ln_b_b"],
      lw["proj_b"], lw["proj_c"], lw["w_out"], lw["ln1_g"], lw["ln1_b"])


def _ffn_call(x2d, lw, layer, *, tile, alpha, out_perm, name):
    rows = x2d.shape[0]
    d_ff = lw["w_ffn_gate"].shape[-1]
    sub_rows = min(FFN_SUB_ROWS, tile)
    assert tile % sub_rows == 0
    x_spec = pl.BlockSpec((tile, D_MODEL), lambda i: (i, 0))
    return pl.pallas_call(
        functools.partial(_ffn_kernel, alpha=alpha, sub_rows=sub_rows, out_perm=out_perm),
        grid=(rows // tile,),
        in_specs=[x_spec,
                  _const_spec((D_MODEL, d_ff), layer), _const_spec((D_MODEL, d_ff), layer),
                  _const_spec((d_ff, D_MODEL), layer),
                  _const_spec((1, D_MODEL), layer), _const_spec((1, D_MODEL), layer)],
        out_specs=x_spec,
        out_shape=jax.ShapeDtypeStruct(x2d.shape, F32),
        scratch_shapes=[pltpu.VMEM((D_MODEL // LANES, sub_rows, LANES) if out_perm else (1, SUBLANES, LANES), F32)],
        compiler_params=pltpu.CompilerParams(dimension_semantics=("arbitrary",),
                                             vmem_limit_bytes=VMEM_LIMIT),
        name=name,
    )(x2d, lw["w_ffn_gate"], lw["w_ffn_up"], lw["w_ffn_down"], lw["ln2_g"], lw["ln2_b"])


def _steps_to_segments(state):
    n_seq, steps, n_ch = state.shape
    return state.transpose(1, 0, 2).reshape(1, steps * n_seq, n_ch)


def _segments_to_steps(state, steps):
    n_ch = state.shape[-1]
    return state.reshape(steps, SUBLANES, n_ch).transpose(1, 0, 2)


def kernel(x_prompt, x_sample, mem_prompt, state_conv_a, state_lru, state_conv_b, cache_mem_k, cache_mem_v,
           ln_in_g, ln_in_b, w_in, b_gate, conv_a_w, conv_a_b, lru_w_r, lru_b_r, lru_w_i, lru_b_i, lru_lambda,
           proj_a, conv_b_w, conv_b_b, ln_b_g, ln_b_b, proj_b, w_mem_k, w_mem_v, proj_c, w_out, ln1_g, ln1_b,
           w_ffn_gate, w_ffn_up, w_ffn_down, ln2_g, ln2_b):
    depth = w_in.shape[0]
    bp, tp, _ = x_prompt.shape
    bs, ts, _ = x_sample.shape
    assert bs == SUBLANES, "the sample group maps its sequences onto the 8 sublanes"
    alpha = (2 * depth) ** 0.25

    row = lambda p: p.reshape(depth, 1, p.shape[-1])
    rep = lambda p, n: jnp.repeat(p.reshape(depth, -1, p.shape[-1]), n, axis=1)
    half_cols = jnp.concatenate([jnp.full((C_GLU,), 1.0, F32), jnp.full((2 * D_CONV,), 0.5, F32),
                                 jnp.full((D_XATTN,), 1.0, F32), jnp.full((3 * D_MODEL,), 0.5, F32)])
    lw = {
        "w_in": (w_in * half_cols).astype(BF16), "b_gate": 0.5 * b_gate,
        "conv_a_w": rep(0.5 * conv_a_w, SUBLANES), "conv_a_b": rep(0.5 * conv_a_b, SUBLANES),
        "w_ri": jnp.concatenate([lru_w_r, lru_w_i], axis=-1).astype(BF16),
        "lru_b_r": 0.5 * lru_b_r.reshape(depth, 1, D_LRU), "lru_b_i": 0.5 * lru_b_i.reshape(depth, 1, D_LRU),
        "lru_lambda": row(lru_lambda), "proj_a": (0.5 * proj_a).astype(BF16),
        "conv_b_w": rep(conv_b_w, SUBLANES), "conv_b_b": rep(conv_b_b, SUBLANES),
        "ln_b_g": 0.5 * row(ln_b_g), "ln_b_b": 0.5 * row(ln_b_b),
        "proj_b": proj_b.astype(BF16), "proj_c": proj_c.astype(BF16), "w_out": (0.5 * w_out).astype(BF16),
        "ln1_g": row(ln1_g), "ln1_b": row(ln1_b),
        "w_ffn_gate": (0.5 * w_ffn_gate).astype(BF16), "w_ffn_up": w_ffn_up.astype(BF16),
        "w_ffn_down": w_ffn_down.astype(BF16), "ln2_g": row(ln2_g), "ln2_b": row(ln2_b),
    }
    ln_in = (ln_in_g.reshape(1, D_MODEL), ln_in_b.reshape(1, D_MODEL))

    mem_k, mem_v, mem_kb, mem_vb = _kv_call(mem_prompt, w_mem_k.astype(BF16), w_mem_v.astype(BF16))
    cache_k = cache_mem_k.reshape(depth, bs, N_MEM, D_XATTN)
    cache_v = cache_mem_v.reshape(depth, bs, N_MEM, D_XATTN)

    sp = 32
    assert tp % (SUBLANES * sp) == 0 and FFN_SUB_ROWS == SUBLANES * sp
    n_tiles = tp // (SUBLANES * sp)
    xp = x_prompt.reshape(bp * tp, D_MODEL)
    xs = x_sample.transpose(1, 0, 2).reshape(ts * bs, D_MODEL)

    zero_ca = jnp.zeros((bp, SUBLANES * HIST_A, D_LRU), F32)
    zero_h = jnp.zeros((bp, SUBLANES, D_LRU), F32)
    zero_cb = jnp.zeros((bp, SUBLANES * HIST_B, D_CONV), F32)

    ca_p, h_p, cb_p, ca_s, h_s, cb_s = [], [], [], [], [], []
    for l in range(depth):
        first, last = l == 0, l == depth - 1
        xp, ca, hl, cb = _mixer_call(xp, mem_kb[l], mem_vb[l], zero_ca, zero_h, zero_cb, ln_in, lw, l,
                                     chained=True, n_blocks=bp, n_tiles=n_tiles, n_vrows=sp, alpha=alpha,
                                     in_ln=first, in_perm=first, name="mixer_prompt")
        xp = _ffn_call(xp, lw, l, tile=_row_tile(bp * tp, 1024), alpha=alpha, out_perm=last, name="ffn_prompt")
        ca_p.append(ca.reshape(bp, HIST_A, SUBLANES, D_LRU)[:, :, SUBLANES - 1])
        h_p.append(hl[:, SUBLANES - 1])
        cb_p.append(cb.reshape(bp, HIST_B, SUBLANES, D_CONV)[:, :, SUBLANES - 1])
        xs, ca, hl, cb = _mixer_call(xs, cache_k[l], cache_v[l], _steps_to_segments(state_conv_a[l]),
                                     state_lru[l][None], _steps_to_segments(state_conv_b[l]), ln_in, lw, l,
                                     chained=False, n_blocks=1, n_tiles=1, n_vrows=ts, alpha=alpha,
                                     in_ln=first, in_perm=False, name="mixer_sample")
        xs = _ffn_call(xs, lw, l, tile=bs * ts, alpha=alpha, out_perm=False, name="ffn_sample")
        ca_s.append(_segments_to_steps(ca[0], HIST_A))
        h_s.append(hl[0])
        cb_s.append(_segments_to_steps(cb[0], HIST_B))

    y_prompt = xp.reshape(bp, tp, D_MODEL)
    y_sample = xs.reshape(ts, bs, D_MODEL).transpose(1, 0, 2)
    kv_shape = (depth, bp, N_MEM, N_HEADS, HEAD_DIM)
    return (y_prompt, y_sample,
            jnp.stack(ca_p), jnp.stack(h_p), jnp.stack(cb_p),
            mem_k.reshape(kv_shape), mem_v.reshape(kv_shape),
            jnp.stack(ca_s), jnp.stack(h_s), jnp.stack(cb_s))
```
